```python
import math, functools
import jax, jax.numpy as jnp
from jax import lax
import numpy as np

D_MODEL = 1024
BATCH = 4
SEQ = 4096
DEPTH = 4
DEC_BATCH = 128
DEC_SEQ = 4
PAST_LEN = 2048
PAGE_SIZE = 128

HG_HEADS = 4
HG_DK = 128
HG_DV = 128
HG_WIDTH = HG_HEADS * HG_DV
HG_CHUNK = 64
DA_HEADS = 4
DA_DK = 64
DA_DV = 2 * DA_DK
DA_WIDTH = DA_HEADS * DA_DV
Q_BLOCK = 128
MEM_LEN = 256
MEM_HEADS = 4
MEM_DH = 128
MEM_WIDTH = MEM_HEADS * MEM_DH
N_BRANCH = 3
IN_COLS = 2 * HG_HEADS * HG_DK + 2 * HG_WIDTH + 4 * DA_HEADS * DA_DK + DA_WIDTH + MEM_WIDTH + N_BRANCH * D_MODEL
PK_HEADS = 8
PK_DQ = 128
PK_DK = PK_DQ // 2
N_KEYS = 128
N_EXPERTS = N_KEYS * N_KEYS
PK_TOPK = 16
PEER_BLOCK = 128
EPS = 1e-6
MASK_VALUE = -1e30

kernel_name = "hybrid_hgrn2_diffattn_peer_step"


def rmsnorm(x, g):
    xf = x.astype(jnp.float32)
    y = xf * lax.rsqrt(jnp.mean(xf * xf, axis=-1, keepdims=True) + EPS)
    return (y * g.astype(jnp.float32)).astype(x.dtype)


def split_proj(proj):
    sizes = (HG_HEADS * HG_DK, HG_HEADS * HG_DK, HG_WIDTH, HG_WIDTH,
             DA_HEADS * 2 * DA_DK, DA_HEADS * 2 * DA_DK, DA_WIDTH, MEM_WIDTH)
    idx = [int(i) for i in np.cumsum(sizes)]
    return jnp.split(proj, idx, axis=-1)


def hgrn2_recurrence(q, k, i, log_f, s0):
    B, T, H, K = q.shape
    C = math.gcd(T, HG_CHUNK)
    n = T // C
    chunks = lambda a: a.reshape(B, n, C, *a.shape[2:]).swapaxes(0, 1)
    causal = jnp.tril(jnp.ones((C, C), bool))[None, :, :, None, None]

    def step(S, xs):
        qc, kc, ic, lc = xs
        b = jnp.cumsum(lc, axis=1)
        o_inter = jnp.einsum('bchk,bhkv->bchv', qc * jnp.exp(b), S)
        diff = jnp.where(causal, b[:, :, None] - b[:, None, :], 0.0)
        dec = jnp.where(causal, jnp.exp(diff), 0.0)
        att = jnp.einsum('bthk,bshk,btshk->bhts', qc, kc, dec)
        o_intra = jnp.einsum('bhts,bshv->bthv', att, ic)
        b_last = b[:, -1]
        S = jnp.exp(b_last)[..., None] * S + jnp.einsum('bshk,bshv->bhkv', kc * jnp.exp(b_last[:, None] - b), ic)
        return S, o_inter + o_intra

    S, o = lax.scan(step, s0, (chunks(q), chunks(k), chunks(i), chunks(log_f)))
    return o.swapaxes(0, 1).reshape(B, T, H, -1), S


def hgrn_branch(hq, hf, hi, hg, lb, onorm_g, s0):
    B, T, _ = hq.shape
    shp = (B, T, HG_HEADS, HG_DK)
    z = hf.astype(jnp.float32).reshape(shp)
    lb = jnp.clip(lb.astype(jnp.float32).reshape(HG_HEADS, HG_DK), 0.0, 1.0 - 1e-4)
    f = lb + (1.0 - lb) * jax.nn.sigmoid(z)
    log_f = jnp.log(f)
    k = (1.0 - lb) * jax.nn.sigmoid(-z)
    q = jax.nn.silu(hq.astype(jnp.float32)).reshape(shp)
    i = hi.astype(jnp.float32).reshape(B, T, HG_HEADS, HG_DV)
    o, s_new = hgrn2_recurrence(q, k, i, log_f, s0.astype(jnp.float32))
    gate = jax.nn.silu(hg.astype(jnp.float32)).reshape(B, T, HG_HEADS, HG_DV)
    o = rmsnorm(o, onorm_g) * gate
    return o.reshape(B, T, HG_WIDTH).astype(hq.dtype), s_new.astype(s0.dtype)


def diff_core(q, keys, values, masks, lam):
    scale = DA_DK ** -0.5
    s = jnp.concatenate(
        [jnp.where(m, jnp.einsum('bqhmd,bkhmd->bhmqk', q, k).astype(jnp.float32) * scale, MASK_VALUE)
         for k, m in zip(keys, masks)], axis=-1)
    pr = jax.nn.softmax(s, axis=-1)
    a = (pr[:, :, 0] - lam * pr[:, :, 1]).astype(q.dtype)
    o = 0.0
    off = 0
    for v in values:
        n = v.shape[1]
        o = o + jnp.einsum('bhqk,bkhd->bqhd', a[..., off:off + n], v)
        off += n
    return o


def diff_attn_prompt(q, k, v, lam):
    B, T = q.shape[:2]
    blk = math.gcd(T, Q_BLOCK)
    nb = T // blk
    qb = q.reshape(B, nb, blk, DA_HEADS, 2, DA_DK).swapaxes(0, 1)
    k_pos = jnp.arange(T)

    def one(args):
        qi, bi = args
        q_pos = bi * blk + jnp.arange(blk)
        return diff_core(qi, [k], [v], [k_pos[None, :] <= q_pos[:, None]], lam)

    o = lax.map(one, (qb, jnp.arange(nb)))
    return o.swapaxes(0, 1).reshape(B, T, DA_HEADS, DA_DV)


def diff_attn_sample(q, k, v, lam, k_past, v_past):
    T = q.shape[1]
    P = k_past.shape[1]
    return diff_core(q, [k_past.astype(q.dtype), k], [v_past.astype(v.dtype), v],
                     [jnp.ones((T, P), bool), jnp.tril(jnp.ones((T, T), bool))], lam)


def mem_kv(mem, g, w, kn_g):
    B, M, _ = mem.shape
    mk, mv = jnp.split(rmsnorm(mem, g) @ w, 2, axis=-1)
    mk = rmsnorm(mk.reshape(B, M, MEM_HEADS, MEM_DH), kn_g)
    return mk, mv.reshape(B, M, MEM_HEADS, MEM_DH)


def cross_attn(q, mk, mv):
    s = jnp.einsum('bthd,bmhd->bhtm', q, mk.astype(q.dtype)).astype(jnp.float32) * MEM_DH ** -0.5
    pr = jax.nn.softmax(s, axis=-1).astype(q.dtype)
    return jnp.einsum('bhtm,bmhd->bthd', pr, mv.astype(q.dtype))


def peer_ffn(h, wq, subkeys, u_tab, v_tab):
    shape = h.shape
    hf = h.reshape(-1, D_MODEL)
    n = hf.shape[0]
    n_pad = -(-n // PEER_BLOCK) * PEER_BLOCK
    hf = jnp.pad(hf, ((0, n_pad - n), (0, 0)))

    def block(hb):
        qry = (hb @ wq).reshape(PEER_BLOCK, PK_HEADS, 2, PK_DK)
        sc = jnp.einsum('nhpd,hpkd->nhpk', qry, subkeys.astype(hb.dtype)).astype(jnp.float32)
        sv, si = lax.top_k(sc, PK_TOPK)
        cand = (sv[:, :, 0, :, None] + sv[:, :, 1, None, :]).reshape(PEER_BLOCK, PK_HEADS, PK_TOPK * PK_TOPK)
        cid = (si[:, :, 0, :, None] * N_KEYS + si[:, :, 1, None, :]).reshape(PEER_BLOCK, PK_HEADS, PK_TOPK * PK_TOPK)
        fv, fi = lax.top_k(cand, PK_TOPK)
        eid = jnp.take_along_axis(cid, fi, axis=-1)
        gate = jax.nn.softmax(fv, axis=-1)
        act = jax.nn.gelu(jnp.einsum('nd,nhkd->nhk', hb, u_tab[eid]).astype(jnp.float32), approximate=False)
        return jnp.einsum('nhk,nhkd->nd', (gate * act).astype(hb.dtype), v_tab[eid])

    out = lax.map(block, hf.reshape(-1, PEER_BLOCK, D_MODEL))
    return out.reshape(n_pad, D_MODEL)[:n].reshape(shape)


def layer(x, p, lb, lam, lam_init, s0, attend, mem_k, mem_v):
    B, T, _ = x.shape
    h = rmsnorm(x, p['norm_mix_g'])
    hq, hf, hi, hg, dq, dk, dv, mq, gpre = split_proj(h @ p['w_in'])
    hg_out, s_new = hgrn_branch(hq, hf, hi, hg, lb, p['hg_onorm_g'], s0)
    q = rmsnorm(dq.reshape(B, T, DA_HEADS, 2, DA_DK), p['da_qnorm_g'])
    k = rmsnorm(dk.reshape(B, T, DA_HEADS, 2, DA_DK), p['da_knorm_g'])
    v = dv.reshape(B, T, DA_HEADS, DA_DV)
    da = attend(q, k, v, lam)
    da_out = (rmsnorm(da, p['da_subln_g']) * (1.0 - lam_init)).reshape(B, T, DA_WIDTH)
    qm = rmsnorm(mq.reshape(B, T, MEM_HEADS, MEM_DH), p['mem_qnorm_g'])
    cm_out = cross_attn(qm, mem_k, mem_v).reshape(B, T, MEM_WIDTH)
    g_hg, g_da, g_cm = jnp.split(jax.nn.sigmoid(gpre.astype(jnp.float32)).astype(x.dtype), N_BRANCH, axis=-1)
    merged = (g_hg * (hg_out @ p['w_branch_hg']) + g_da * (da_out @ p['w_branch_da'])
              + g_cm * (cm_out @ p['w_branch_mem']))
    x = x + merged @ p['w_out']
    x = x + peer_ffn(rmsnorm(x, p['norm_ffn_g']), p['peer_wq'], p['peer_subkeys'], p['peer_u'], p['peer_v'])
    return x, s_new, k, v


def setup_inputs(seed: int = 0) -> dict:
    key = jax.random.key(seed)
    ks = jax.random.split(key, 40)
    counter = [0]

    def nk():
        counter[0] += 1
        return ks[counter[0] - 1]

    def nrm(shape, scale):
        return jax.random.normal(nk(), shape, jnp.float32) * scale

    def gain(shape):
        return 1.0 + 0.02 * jax.random.normal(nk(), shape, jnp.float32)

    n_pages = PAST_LEN // PAGE_SIZE
    n_used = DEC_BATCH * n_pages
    n_pool = n_used + n_used // 4
    page_table = jax.random.permutation(nk(), n_pool)[:n_used].reshape(DEC_BATCH, n_pages).astype(jnp.int32)
    dinv = D_MODEL ** -0.5
    return {
        'x_prompt': nrm((BATCH, SEQ, D_MODEL), 1.0),
        'x_sample': nrm((DEC_BATCH, DEC_SEQ, D_MODEL), 1.0),
        'cache_diff_k': nrm((DEPTH, n_pool, PAGE_SIZE, DA_HEADS, 2, DA_DK), 1.0),
        'cache_diff_v': nrm((DEPTH, n_pool, PAGE_SIZE, DA_HEADS, DA_DV), 1.0),
        'cache_mem_k': nrm((DEPTH, DEC_BATCH, MEM_LEN, MEM_HEADS, MEM_DH), 1.0),
        'cache_mem_v': nrm((DEPTH, DEC_BATCH, MEM_LEN, MEM_HEADS, MEM_DH), 1.0),
        'state_hgrn': nrm((DEPTH, DEC_BATCH, HG_HEADS, HG_DK, HG_DV), 0.5),
        'page_table': page_table,
        'mem_prompt': nrm((BATCH, MEM_LEN, D_MODEL), 1.0),
        'norm_mix_g': gain((DEPTH, D_MODEL)),
        'w_in': nrm((DEPTH, D_MODEL, IN_COLS), dinv),
        'hg_lb_logits': nrm((DEPTH, HG_HEADS * HG_DK), 0.1),
        'hg_onorm_g': gain((DEPTH, HG_DV)),
        'da_qnorm_g': gain((DEPTH, DA_DK)),
        'da_knorm_g': gain((DEPTH, DA_DK)),
        'da_lambda': nrm((DEPTH, 4, DA_DK), 0.1),
        'da_subln_g': gain((DEPTH, DA_DV)),
        'mem_norm_g': gain((DEPTH, D_MODEL)),
        'w_mem_kv': nrm((DEPTH, D_MODEL, 2 * MEM_WIDTH), dinv),
        'mem_qnorm_g': gain((DEPTH, MEM_DH)),
        'mem_knorm_g': gain((DEPTH, MEM_DH)),
        'w_branch_hg': nrm((DEPTH, HG_WIDTH, D_MODEL), HG_WIDTH ** -0.5),
        'w_branch_da': nrm((DEPTH, DA_WIDTH, D_MODEL), DA_WIDTH ** -0.5),
        'w_branch_mem': nrm((DEPTH, MEM_WIDTH, D_MODEL), MEM_WIDTH ** -0.5),
        'w_out': nrm((DEPTH, D_MODEL, D_MODEL), dinv),
        'norm_ffn_g': gain((DEPTH, D_MODEL)),
        'peer_wq': nrm((DEPTH, D_MODEL, PK_HEADS * PK_DQ), dinv),
        'peer_subkeys': nrm((DEPTH, PK_HEADS, 2, N_KEYS, PK_DK), PK_DK ** -0.5),
        'peer_u': nrm((DEPTH, N_EXPERTS, D_MODEL), dinv),
        'peer_v': nrm((DEPTH, N_EXPERTS, D_MODEL), 0.1),
    }


def reference(x_prompt, x_sample, cache_diff_k, cache_diff_v, cache_mem_k, cache_mem_v, state_hgrn,
              page_table, mem_prompt, norm_mix_g, w_in, hg_lb_logits, hg_onorm_g, da_qnorm_g, da_knorm_g,
              da_lambda, da_subln_g, mem_norm_g, w_mem_kv, mem_qnorm_g, mem_knorm_g, w_branch_hg,
              w_branch_da, w_branch_mem, w_out, norm_ffn_g, peer_wq, peer_subkeys, peer_u, peer_v):
    pr_lb = jax.nn.softmax(hg_lb_logits.astype(jnp.float32), axis=0)
    lb_all = jnp.cumsum(pr_lb, axis=0) - pr_lb[0:1]
    n_p = x_prompt.shape[0]
    n_s = x_sample.shape[0]
    xp, xs = x_prompt, x_sample
    hg_p, hg_s, dk_p, dv_p, dk_s, dv_s, mk_list, mv_list = [], [], [], [], [], [], [], []
    for l in range(DEPTH):
        p = {
            'norm_mix_g': norm_mix_g[l], 'w_in': w_in[l], 'hg_onorm_g': hg_onorm_g[l],
            'da_qnorm_g': da_qnorm_g[l], 'da_knorm_g': da_knorm_g[l], 'da_subln_g': da_subln_g[l],
            'mem_qnorm_g': mem_qnorm_g[l], 'w_branch_hg': w_branch_hg[l], 'w_branch_da': w_branch_da[l],
            'w_branch_mem': w_branch_mem[l], 'w_out': w_out[l], 'norm_ffn_g': norm_ffn_g[l],
            'peer_wq': peer_wq[l], 'peer_subkeys': peer_subkeys[l], 'peer_u': peer_u[l], 'peer_v': peer_v[l],
        }
        lam_init = 0.8 - 0.6 * math.exp(-0.3 * l)
        lp = da_lambda[l].astype(jnp.float32)
        lam = jnp.exp(jnp.sum(lp[0] * lp[1])) - jnp.exp(jnp.sum(lp[2] * lp[3])) + lam_init
        mk_p, mv_p = mem_kv(mem_prompt, mem_norm_g[l], w_mem_kv[l], mem_knorm_g[l])
        s0_p = jnp.zeros((n_p, HG_HEADS, HG_DK, HG_DV), xp.dtype)
        xp, s_p, k_p, v_p = layer(xp, p, lb_all[l], lam, lam_init, s0_p, diff_attn_prompt, mk_p, mv_p)
        k_past = cache_diff_k[l, page_table].reshape(n_s, -1, DA_HEADS, 2, DA_DK)
        v_past = cache_diff_v[l, page_table].reshape(n_s, -1, DA_HEADS, DA_DV)
        attend_s = functools.partial(diff_attn_sample, k_past=k_past, v_past=v_past)
        xs, s_s, k_s, v_s = layer(xs, p, lb_all[l], lam, lam_init, state_hgrn[l], attend_s,
                                  cache_mem_k[l], cache_mem_v[l])
        hg_p.append(s_p)
        hg_s.append(s_s)
        dk_p.append(k_p)
        dv_p.append(v_p)
        dk_s.append(k_s)
        dv_s.append(v_s)
        mk_list.append(mk_p)
        mv_list.append(mv_p)
    return (xp, xs, jnp.stack(hg_p), jnp.stack(hg_s), jnp.stack(dk_p), jnp.stack(dv_p),
            jnp.stack(dk_s), jnp.stack(dv_s), jnp.stack(mk_list), jnp.stack(mv_list))
```

```python
import functools
import math

import numpy as np
import jax
import jax.numpy as jnp
from jax import lax
from jax.experimental import pallas as pl
from jax.experimental.pallas import tpu as pltpu

F32 = jnp.float32
BF16 = jnp.bfloat16
I32 = jnp.int32

D_MODEL = 1024
DEPTH = 4
PAGE_SIZE = 128
HEADS = 4
HEAD_W = 128
BRANCH_W = HEADS * HEAD_W
DA_DK = 64
MEM_LEN = 256
N_BRANCH = 3
IN_COLS = 8 * BRANCH_W + N_BRANCH * D_MODEL
PK_HEADS = 8
N_KEYS = 128
N_EXPERTS = N_KEYS * N_KEYS
PK_TOPK = 16
EPS = 1e-6
MASK_VALUE = -1e30
NEG_INF = float("-inf")

LANES = 128
NT_DIMS = (((1,), (1,)), ((), ()))
TN_DIMS = (((0,), (0,)), ((), ()))

_CAND_ROWS = 80


def _cparams(sem, vmem_mb=None):
    kw = dict(dimension_semantics=sem)
    if vmem_mb is not None:
        kw["vmem_limit_bytes"] = vmem_mb * 2 ** 20
    return pltpu.CompilerParams(**kw)


def _split_dot(x, m_bf16, terms, left=False):
    acc = None
    r = x
    for _ in range(terms):
        p = r.astype(BF16)
        r = r - p.astype(F32)
        if left:
            y = jnp.dot(m_bf16, p, preferred_element_type=F32)
        else:
            y = jnp.dot(p, m_bf16, preferred_element_type=F32)
        acc = y if acc is None else acc + y
    return acc


def _sigmoid(x):
    return 1.0 / (1.0 + jnp.exp(-x))


def _norm_matmul_kernel(x_ref, g_ref, w_ref, o_ref, hn_ref):
    @pl.when(pl.program_id(1) == 0)
    def _():
        x = x_ref[...]
        ms = jnp.mean(x * x, axis=-1, keepdims=True)
        hn_ref[...] = ((x * lax.rsqrt(ms + EPS)) * g_ref[...]).astype(BF16)

    o_ref[...] = jnp.dot(hn_ref[...], w_ref[...], preferred_element_type=F32)


def _norm_matmul(x, g, w, *, tn, tc):
    n, d = x.shape
    c = w.shape[1]
    return pl.pallas_call(
        _norm_matmul_kernel,
        grid=(n // tn, c // tc),
        in_specs=[
            pl.BlockSpec((tn, d), lambda i, j: (i, 0)),
            pl.BlockSpec((1, d), lambda i, j: (0, 0)),
            pl.BlockSpec((d, tc), lambda i, j: (0, j)),
        ],
        out_specs=pl.BlockSpec((tn, tc), lambda i, j: (i, j)),
        out_shape=jax.ShapeDtypeStruct((n, c), F32),
        scratch_shapes=[pltpu.VMEM((tn, d), BF16)],
        compiler_params=_cparams(("parallel", "arbitrary")),
        name="norm_matmul",
    )(x, g.reshape(1, d), w)


def _group_mean_matrix(width, group):
    r = np.arange(width)
    return jnp.asarray((r[:, None] // group == r[None, :] // group) / group, BF16)


def _prep_kernel(dq_ref, dk_ref, dv_ref, mq_ref, gq_ref, gk_ref, gm_ref, bd64_ref, bd128_ref,
                 qn_ref, kn_ref, knb_ref, vb_ref, qm_ref):
    def gnorm(x, bd, g):
        ms = _split_dot(x * x, bd, 3)
        return (x * lax.rsqrt(ms + EPS)) * g

    q = gnorm(dq_ref[...], bd64_ref[...], gq_ref[...])
    qn_ref[...] = (q * (DA_DK ** -0.5)).astype(BF16)
    k = gnorm(dk_ref[...], bd64_ref[...], gk_ref[...])
    kn_ref[...] = k
    knb_ref[...] = k.astype(BF16)
    vb_ref[...] = dv_ref[...].astype(BF16)
    qm_ref[...] = gnorm(mq_ref[...], bd128_ref[...], gm_ref[...]).astype(BF16)


def _prep(proj, gq, gk, gm, *, tn):
    n = proj.shape[0]
    w = BRANCH_W
    col = lambda c: pl.BlockSpec((tn, w), lambda i, c=c: (i, c))
    const = lambda shape: pl.BlockSpec(shape, lambda i: (0, 0))
    row_out = pl.BlockSpec((tn, w), lambda i: (i, 0))
    tile = lambda g: jnp.tile(g.astype(F32), w // g.shape[0]).reshape(1, w)
    return pl.pallas_call(
        _prep_kernel,
        grid=(n // tn,),
        in_specs=[col(4), col(5), col(6), col(7), const((1, w)), const((1, w)), const((1, w)),
                  const((w, w)), const((w, w))],
        out_specs=[row_out] * 5,
        out_shape=[jax.ShapeDtypeStruct((n, w), BF16), jax.ShapeDtypeStruct((n, w), F32),
                   jax.ShapeDtypeStruct((n, w), BF16), jax.ShapeDtypeStruct((n, w), BF16),
                   jax.ShapeDtypeStruct((n, w), BF16)],
        compiler_params=_cparams(("parallel",)),
        name="qk_norm",
    )(proj, proj, proj, proj, tile(gq), tile(gk), tile(gm),
      _group_mean_matrix(w, DA_DK), _group_mean_matrix(w, HEAD_W))


def _hgrn_kernel(*refs, layer, nb, tc, c, tvalid, has_s0):
    if has_s0:
        hq_ref, hf_ref, hi_ref, hg_ref, lbl_ref, og_ref, tri_ref, s0_ref = refs[:8]
        rest = refs[8:]
    else:
        hq_ref, hf_ref, hi_ref, hg_ref, lbl_ref, og_ref, tri_ref = refs[:7]
        s0_ref = None
        rest = refs[7:]
    o_ref, sout_ref, st_ref, q_s, k_s, b_s = rest
    t = pl.program_id(1)
    rows = nb * tc

    @pl.when(t == 0)
    def _():
        if has_s0:
            for ib in range(nb):
                for h in range(HEADS):
                    st_ref[ib, h] = s0_ref[ib, h].T
        else:
            st_ref[...] = jnp.zeros_like(st_ref)

    lg = lbl_ref[...]
    e = jnp.exp(lg - jnp.max(lg, axis=0, keepdims=True))
    pr = e / jnp.sum(e, axis=0, keepdims=True)
    lb = jnp.zeros((1, BRANCH_W), F32)
    for i in range(1, layer + 1):
        lb = lb + pr[i:i + 1]
    lb = jnp.clip(lb, 0.0, 1.0 - 1e-4)

    z = hf_ref[...]
    sg = _sigmoid(z)
    logf = jnp.log(lb + (1.0 - lb) * sg)
    kk = (1.0 - lb) * _sigmoid(-z)
    if tvalid < c:
        valid = (lax.broadcasted_iota(I32, (rows, BRANCH_W), 0) % c) < tvalid
        logf = jnp.where(valid, logf, 0.0)
        kk = jnp.where(valid, kk, 0.0)
    hq = hq_ref[...]
    q_s[...] = hq * _sigmoid(hq)
    k_s[...] = kk
    b_s[...] = _split_dot(logf, tri_ref[...], 3, left=True)

    og = og_ref[...]
    row_id = lax.broadcasted_iota(I32, (c, HEAD_W), 0)
    nsub = tc // c

    def sub(i, carry):
        r0 = pl.multiple_of(i * c, c)
        ib = i // nsub
        first = (i % nsub) == 0
        new_carry = []
        for h in range(HEADS):
            sl = slice(h * HEAD_W, (h + 1) * HEAD_W)
            bg = b_s[pl.ds(r0, c), sl]
            new_carry.append(bg[c - 1:c, :])
            b = bg - jnp.where(first, 0.0, carry[h])
            q = q_s[pl.ds(r0, c), sl]
            kx = k_s[pl.ds(r0, c), sl]
            iv = hi_ref[pl.ds(r0, c), sl]
            st = st_ref[ib, h]
            o_inter = lax.dot_general((q * jnp.exp(b)).astype(BF16), st.astype(BF16), NT_DIMS,
                                      preferred_element_type=F32)
            o_rows = []
            for tt in range(c):
                d = b[tt:tt + 1, :] - b
                w = jnp.exp(jnp.where(row_id <= tt, d, MASK_VALUE))
                p = (q[tt:tt + 1, :] * kx) * w
                a = jnp.sum(p, axis=-1, keepdims=True)
                o_rows.append(jnp.sum(a * iv, axis=0, keepdims=True))
            o = o_inter + jnp.concatenate(o_rows, axis=0)
            bl = b[c - 1:c, :]
            kd = kx * jnp.exp(bl - b)
            upd = lax.dot_general(iv.astype(BF16), kd.astype(BF16), TN_DIMS, preferred_element_type=F32)
            st_ref[ib, h] = jnp.exp(bl) * st + upd
            ms = jnp.mean(o * o, axis=-1, keepdims=True)
            hg = hg_ref[pl.ds(r0, c), sl]
            o_ref[pl.ds(r0, c), sl] = (((o * lax.rsqrt(ms + EPS)) * og) * (hg * _sigmoid(hg))).astype(o_ref.dtype)
        return tuple(new_carry)

    lax.fori_loop(0, nb * nsub, sub, tuple(jnp.zeros((1, HEAD_W), F32) for _ in range(HEADS)))

    @pl.when(t == pl.num_programs(1) - 1)
    def _():
        for ib in range(nb):
            for h in range(HEADS):
                sout_ref[ib, h] = st_ref[ib, h].T


def _hgrn(src, lb_logits, onorm_g, s0_all, *, layer, n_seq, seq_len, row0, nb, tc, c, tvalid, out_dtype):
    rows = nb * tc
    nt = seq_len // tc
    rb0 = row0 // rows
    w = BRANCH_W
    col = lambda cc: pl.BlockSpec((rows, w), lambda b, t, cc=cc: (rb0 + b * nt + t, cc))
    r = np.arange(rows)
    tri = jnp.asarray((r[:, None] >= r[None, :]) & (r[:, None] // tc == r[None, :] // tc), BF16)
    in_specs = [col(0), col(1), col(2), col(3),
                pl.BlockSpec((DEPTH, w), lambda b, t: (0, 0)),
                pl.BlockSpec((1, HEAD_W), lambda b, t: (0, 0)),
                pl.BlockSpec((rows, rows), lambda b, t: (0, 0))]
    args = [src, src, src, src, lb_logits, onorm_g.reshape(1, HEAD_W), tri]
    has_s0 = s0_all is not None
    if has_s0:
        in_specs.append(pl.BlockSpec((None, nb, HEADS, HEAD_W, HEAD_W), lambda b, t: (layer, b, 0, 0, 0)))
        args.append(s0_all)
    kern = functools.partial(_hgrn_kernel, layer=layer, nb=nb, tc=tc, c=c, tvalid=tvalid, has_s0=has_s0)
    return pl.pallas_call(
        kern,
        grid=(n_seq // nb, nt),
        in_specs=in_specs,
        out_specs=[pl.BlockSpec((rows, w), lambda b, t: (b * nt + t, 0)),
                   pl.BlockSpec((nb, HEADS, HEAD_W, HEAD_W), lambda b, t: (b, 0, 0, 0))],
        out_shape=[jax.ShapeDtypeStruct((n_seq * seq_len, w), out_dtype),
                   jax.ShapeDtypeStruct((n_seq, HEADS, HEAD_W, HEAD_W), F32)],
        scratch_shapes=[pltpu.VMEM((nb, HEADS, HEAD_W, HEAD_W), F32),
                        pltpu.VMEM((rows, w), F32), pltpu.VMEM((rows, w), F32), pltpu.VMEM((rows, w), F32)],
        compiler_params=_cparams(("parallel", "arbitrary")),
        name="hgrn2",
    )(*args)


def _lambda_value(lp, lam_init):
    a = jnp.sum(lp[0:1] * lp[1:2], axis=-1, keepdims=True)
    b = jnp.sum(lp[2:3] * lp[3:4], axis=-1, keepdims=True)
    return jnp.exp(a) - jnp.exp(b) + lam_init


def _dattn_prompt_kernel(lam_ref, q_ref, k_ref, v_ref, sg_ref, o_ref, q2_s, m_s, l_s, acc_s, *, tq, lam_init):
    iq = pl.program_id(2)
    ik = pl.program_id(3)

    @pl.when(ik == 0)
    def _():
        q = q_ref[...].astype(F32)
        lane = lax.broadcasted_iota(I32, q.shape, 1)
        q2_s[0:tq, :] = jnp.where(lane < DA_DK, q, 0.0).astype(BF16)
        q2_s[tq:2 * tq, :] = jnp.where(lane >= DA_DK, q, 0.0).astype(BF16)
        m_s[...] = jnp.full_like(m_s, NEG_INF)
        l_s[...] = jnp.zeros_like(l_s)
        acc_s[...] = jnp.zeros_like(acc_s)

    def update(masked):
        s = lax.dot_general(q2_s[...], k_ref[...], NT_DIMS, preferred_element_type=F32)
        if masked:
            r = lax.broadcasted_iota(I32, s.shape, 0)
            r = jnp.where(r >= tq, r - tq, r)
            cidx = lax.broadcasted_iota(I32, s.shape, 1)
            s = jnp.where(cidx <= r, s, MASK_VALUE)
        m_prev = m_s[...]
        m_new = jnp.maximum(m_prev, jnp.max(s, axis=-1, keepdims=True))
        alpha = jnp.exp(m_prev - m_new)
        p = jnp.exp(s - m_new)
        l_s[...] = alpha * l_s[...] + jnp.sum(p, axis=-1, keepdims=True)
        acc_s[...] = alpha * acc_s[...] + jnp.dot(p.astype(BF16), v_ref[...], preferred_element_type=F32)
        m_s[...] = m_new

    @pl.when(ik < iq)
    def _():
        update(False)

    @pl.when(ik == iq)
    def _():
        update(True)
        lam = _lambda_value(lam_ref[...], lam_init)
        o0 = acc_s[0:tq, :] / l_s[0:tq, :]
        o1 = acc_s[tq:2 * tq, :] / l_s[tq:2 * tq, :]
        da = o0 - lam * o1
        ms = jnp.mean(da * da, axis=-1, keepdims=True)
        o_ref[...] = (((da * lax.rsqrt(ms + EPS)) * sg_ref[...]) * (1.0 - lam_init)).astype(o_ref.dtype)


def _dattn_prompt(qn, knb, vb, lam_p, subln_g, *, n_seq, seq_len, tq, lam_init):
    nq = seq_len // tq
    qspec = pl.BlockSpec((tq, HEAD_W), lambda b, h, iq, ik: (b * nq + iq, h))
    kspec = pl.BlockSpec((tq, HEAD_W), lambda b, h, iq, ik: (b * nq + jnp.minimum(ik, iq), h))
    return pl.pallas_call(
        functools.partial(_dattn_prompt_kernel, tq=tq, lam_init=lam_init),
        grid=(n_seq, HEADS, nq, nq),
        in_specs=[pl.BlockSpec((4, DA_DK), lambda b, h, iq, ik: (0, 0)), qspec, kspec, kspec,
                  pl.BlockSpec((1, HEAD_W), lambda b, h, iq, ik: (0, 0))],
        out_specs=qspec,
        out_shape=jax.ShapeDtypeStruct((n_seq * seq_len, BRANCH_W), BF16),
        scratch_shapes=[pltpu.VMEM((2 * tq, HEAD_W), BF16), pltpu.VMEM((2 * tq, 1), F32),
                        pltpu.VMEM((2 * tq, 1), F32), pltpu.VMEM((2 * tq, HEAD_W), F32)],
        compiler_params=_cparams(("parallel", "parallel", "parallel", "arbitrary")),
        name="diff_attn_prompt",
    )(lam_p, qn, knb, vb, subln_g.reshape(1, HEAD_W))


def _dattn_sample_kernel(pt_ref, lam_ref, qr_ref, kc_ref, vc_ref, kp_ref, vp_ref, sg_ref, o_ref,
                         m_s, l_s, acc_s, *, t_new, lam_init):
    p_idx = pl.program_id(1)
    nrow = 2 * HEADS * t_new

    @pl.when(p_idx == 0)
    def _():
        m_s[...] = jnp.full_like(m_s, NEG_INF)
        l_s[...] = jnp.zeros_like(l_s)
        acc_s[...] = jnp.zeros_like(acc_s)

    qr = qr_ref[...]
    s = lax.dot_general(qr, kp_ref[...].astype(BF16), NT_DIMS, preferred_element_type=F32)
    m_prev = m_s[...]
    m_new = jnp.maximum(m_prev, jnp.max(s, axis=-1, keepdims=True))
    alpha = jnp.exp(m_prev - m_new)
    p = jnp.exp(s - m_new)
    l_s[...] = alpha * l_s[...] + jnp.sum(p, axis=-1, keepdims=True)
    acc_s[...] = alpha * acc_s[...] + jnp.dot(p.astype(BF16), vp_ref[...].astype(BF16),
                                              preferred_element_type=F32)
    m_s[...] = m_new

    @pl.when(p_idx == pl.num_programs(1) - 1)
    def _():
        qf = qr.astype(F32)
        kc = kc_ref[...].astype(F32)
        vc = vc_ref[...].astype(F32)
        t_of_row = lax.broadcasted_iota(I32, (nrow, 1), 0) % t_new
        sc = []
        for j in range(t_new):
            sj = jnp.sum(qf * kc[j:j + 1, :], axis=-1, keepdims=True)
            sc.append(jnp.where(t_of_row >= j, sj, MASK_VALUE))
        m_prev = m_s[...]
        m_new = m_prev
        for sj in sc:
            m_new = jnp.maximum(m_new, sj)
        alpha = jnp.exp(m_prev - m_new)
        l = alpha * l_s[...]
        acc = alpha * acc_s[...]
        for j, sj in enumerate(sc):
            pj = jnp.exp(sj - m_new)
            l = l + pj
            acc = acc + pj * vc[j:j + 1, :]
        on = acc / l
        lam = _lambda_value(lam_ref[...], lam_init)
        for h in range(HEADS):
            sl = slice(h * HEAD_W, (h + 1) * HEAD_W)
            r0 = (2 * h) * t_new
            da = on[r0:r0 + t_new, sl] - lam * on[r0 + t_new:r0 + 2 * t_new, sl]
            ms = jnp.mean(da * da, axis=-1, keepdims=True)
            o_ref[:, sl] = (((da * lax.rsqrt(ms + EPS)) * sg_ref[...]) * (1.0 - lam_init)).astype(o_ref.dtype)


def _dattn_sample(page_table, qrows, kcur, vcur, cache_k, cache_v, lam_p, subln_g, *, layer, lam_init):
    n_seq, n_pages = page_table.shape
    t_new = kcur.shape[1]
    nrow = qrows.shape[1]
    per_seq = lambda shape: pl.BlockSpec((None,) + shape, lambda b, p, pt: (b, 0, 0))
    page = pl.BlockSpec((None, None, PAGE_SIZE, BRANCH_W), lambda b, p, pt: (layer, pt[b * n_pages + p], 0, 0))
    grid_spec = pltpu.PrefetchScalarGridSpec(
        num_scalar_prefetch=1,
        grid=(n_seq, n_pages),
        in_specs=[pl.BlockSpec((4, DA_DK), lambda b, p, pt: (0, 0)),
                  per_seq((nrow, BRANCH_W)), per_seq((t_new, BRANCH_W)), per_seq((t_new, BRANCH_W)),
                  page, page,
                  pl.BlockSpec((1, HEAD_W), lambda b, p, pt: (0, 0))],
        out_specs=per_seq((t_new, BRANCH_W)),
        scratch_shapes=[pltpu.VMEM((nrow, 1), F32), pltpu.VMEM((nrow, 1), F32), pltpu.VMEM((nrow, BRANCH_W), F32)],
    )
    return pl.pallas_call(
        functools.partial(_dattn_sample_kernel, t_new=t_new, lam_init=lam_init),
        grid_spec=grid_spec,
        out_shape=jax.ShapeDtypeStruct((n_seq, t_new, BRANCH_W), F32),
        compiler_params=_cparams(("parallel", "arbitrary")),
        name="diff_attn_sample",
    )(page_table.reshape(-1), lam_p, qrows, kcur, vcur, cache_k, cache_v, subln_g.reshape(1, HEAD_W))


def _mem_kv_kernel(m_ref, g_ref, w_ref, kg_ref, bd_ref, mk_ref, mv_ref):
    x = m_ref[...]
    ms = jnp.mean(x * x, axis=-1, keepdims=True)
    hn = ((x * lax.rsqrt(ms + EPS)) * g_ref[...]).astype(BF16)
    kv = jnp.dot(hn, w_ref[...], preferred_element_type=F32)
    k = kv[:, :BRANCH_W]
    ms = _split_dot(k * k, bd_ref[...], 3)
    mk_ref[...] = (k * lax.rsqrt(ms + EPS)) * kg_ref[...]
    mv_ref[...] = kv[:, BRANCH_W:]


def _mem_kv(mem, norm_g, w_kv, knorm_g):
    n_seq = mem.shape[0]
    w = BRANCH_W
    out = pl.BlockSpec((None, None, MEM_LEN, w), lambda l, b: (l, b, 0, 0))
    return pl.pallas_call(
        _mem_kv_kernel,
        grid=(DEPTH, n_seq),
        in_specs=[pl.BlockSpec((None, MEM_LEN, D_MODEL), lambda l, b: (b, 0, 0)),
                  pl.BlockSpec((None, 1, D_MODEL), lambda l, b: (l, 0, 0)),
                  pl.BlockSpec((None, D_MODEL, 2 * w), lambda l, b: (l, 0, 0)),
                  pl.BlockSpec((None, 1, w), lambda l, b: (l, 0, 0)),
                  pl.BlockSpec((w, w), lambda l, b: (0, 0))],
        out_specs=[out, out],
        out_shape=[jax.ShapeDtypeStruct((DEPTH, n_seq, MEM_LEN, w), F32)] * 2,
        compiler_params=_cparams(("parallel", "parallel")),
        name="mem_kv",
    )(mem, norm_g.reshape(DEPTH, 1, D_MODEL), w_kv,
      jnp.tile(knorm_g.astype(F32), (1, HEADS)).reshape(DEPTH, 1, w), _group_mean_matrix(w, HEAD_W))


def _cross_kernel(q_ref, mk_ref, mv_ref, o_ref):
    q = q_ref[...]
    mk = mk_ref[...].astype(BF16)
    mv = mv_ref[...].astype(BF16)
    for h in range(HEADS):
        sl = slice(h * HEAD_W, (h + 1) * HEAD_W)
        s = lax.dot_general(q[:, sl], mk[:, sl], NT_DIMS, preferred_element_type=F32) * (HEAD_W ** -0.5)
        p = jnp.exp(s - jnp.max(s, axis=-1, keepdims=True))
        l = jnp.sum(p, axis=-1, keepdims=True)
        o = jnp.dot(p.astype(BF16), mv[:, sl], preferred_element_type=F32) / l
        o_ref[:, sl] = o.astype(o_ref.dtype)


def _cross_prompt(qm, mk_all, mv_all, *, layer, n_seq, seq_len, tq):
    nq = seq_len // tq
    mem = pl.BlockSpec((None, None, MEM_LEN, BRANCH_W), lambda b, i: (layer, b, 0, 0))
    row = pl.BlockSpec((tq, BRANCH_W), lambda b, i: (b * nq + i, 0))
    return pl.pallas_call(
        _cross_kernel,
        grid=(n_seq, nq),
        in_specs=[row, mem, mem],
        out_specs=row,
        out_shape=jax.ShapeDtypeStruct((n_seq * seq_len, BRANCH_W), BF16),
        compiler_params=_cparams(("parallel", "parallel")),
        name="cross_attn_prompt",
    )(qm, mk_all, mv_all)


def _cross_sample(qm_pad, mk_all, mv_all, *, layer):
    n_seq, tpad, w = qm_pad.shape
    mem = pl.BlockSpec((None, None, MEM_LEN, w), lambda b: (layer, b, 0, 0))
    row = pl.BlockSpec((None, tpad, w), lambda b: (b, 0, 0))
    return pl.pallas_call(
        _cross_kernel,
        grid=(n_seq,),
        in_specs=[row, mem, mem],
        out_specs=row,
        out_shape=jax.ShapeDtypeStruct((n_seq, tpad, w), F32),
        compiler_params=_cparams(("parallel",)),
        name="cross_attn_sample",
    )(qm_pad, mk_all, mv_all)


def _merge_kernel(x_ref, g0_ref, g1_ref, g2_ref, a0_ref, a1_ref, a2_ref, w0_ref, w1_ref, w2_ref, wo_ref, o_ref):
    merged = None
    for g_ref, a_ref, w_ref in ((g0_ref, a0_ref, w0_ref), (g1_ref, a1_ref, w1_ref), (g2_ref, a2_ref, w2_ref)):
        y = _sigmoid(g_ref[...]) * jnp.dot(a_ref[...], w_ref[...], preferred_element_type=F32)
        merged = y if merged is None else merged + y
    o_ref[...] = x_ref[...] + jnp.dot(merged.astype(BF16), wo_ref[...], preferred_element_type=F32)


def _merge(x, proj, hg, da, cm, w_hg, w_da, w_mem, w_out, *, tn):
    n, d = x.shape
    w = BRANCH_W
    gate_col0 = (8 * w) // d
    row = lambda width: pl.BlockSpec((tn, width), lambda i: (i, 0))
    gate = lambda c: pl.BlockSpec((tn, d), lambda i, c=c: (i, gate_col0 + c))
    const = lambda shape: pl.BlockSpec(shape, lambda i: (0, 0))
    return pl.pallas_call(
        _merge_kernel,
        grid=(n // tn,),
        in_specs=[row(d), gate(0), gate(1), gate(2), row(w), row(w), row(w),
                  const((w, d)), const((w, d)), const((w, d)), const((d, d))],
        out_specs=row(d),
        out_shape=jax.ShapeDtypeStruct((n, d), F32),
        compiler_params=_cparams(("parallel",), vmem_mb=48),
        name="branch_merge",
    )(x, proj, proj, proj, hg, da, cm, w_hg, w_da, w_mem, w_out)


def _topk_rows(s, k, key_id, sentinel):
    vals, ids = [], []
    for _ in range(k):
        m = jnp.max(s, axis=0, keepdims=True)
        sel = jnp.min(jnp.where(s == m, key_id, sentinel), axis=0, keepdims=True)
        s = jnp.where(key_id == sel, NEG_INF, s)
        vals.append(m)
        ids.append(sel)
    return vals, ids


def _candidate_tables(tt):
    flat = np.full((_CAND_ROWS,), 1 << 20, np.int64)
    flat[0:16] = np.arange(16)
    flat[16:24] = 16 + np.arange(8)
    for j1 in range(2, 8):
        lim = PK_TOPK // (j1 + 1)
        flat[24 + (j1 - 2) * 8:24 + (j1 - 2) * 8 + lim] = j1 * 16 + np.arange(lim)
    flat[72:80] = (8 + np.arange(8)) * 16
    return jnp.asarray(np.broadcast_to(flat[:, None], (_CAND_ROWS, tt)), I32)


def _peer_topk_kernel(x_ref, g_ref, wq_ref, sk_ref, cflat_ref, hn_ref, eid_ref, gate_ref, eid_t, gate_t):
    x = x_ref[...]
    ms = jnp.mean(x * x, axis=-1, keepdims=True)
    hn = ((x * lax.rsqrt(ms + EPS)) * g_ref[...]).astype(BF16)
    hn_ref[...] = hn
    tt = x.shape[0]
    key_id = lax.broadcasted_iota(I32, (N_KEYS, tt), 0)
    cflat = cflat_ref[...]
    cvalid = cflat < PK_TOPK * PK_TOPK

    def head(h, carry):
        qh = jnp.dot(hn, wq_ref[h], preferred_element_type=F32).astype(BF16)
        sc = lax.dot_general(sk_ref[h], qh, NT_DIMS, preferred_element_type=F32)
        v1, i1 = _topk_rows(sc[:N_KEYS], PK_TOPK, key_id, N_KEYS)
        v2, i2 = _topk_rows(sc[N_KEYS:], PK_TOPK, key_id, N_KEYS)
        v2a = jnp.concatenate(v2, axis=0)
        i2a = jnp.concatenate(i2, axis=0)
        pieces_v = [v1[0] + v2a, v1[1] + v2a[:8]]
        pieces_e = [i1[0] * N_KEYS + i2a, i1[1] * N_KEYS + i2a[:8]]
        for j1 in range(2, 8):
            pieces_v.append(v1[j1] + v2a[:8])
            pieces_e.append(i1[j1] * N_KEYS + i2a[:8])
        pieces_v.append(jnp.concatenate(v1[8:], axis=0) + v2a[:1])
        pieces_e.append(jnp.concatenate(i1[8:], axis=0) * N_KEYS + i2a[:1])
        cand = jnp.where(cvalid, jnp.concatenate(pieces_v, axis=0), NEG_INF)
        ce = jnp.concatenate(pieces_e, axis=0)
        fv, fe = [], []
        for _ in range(PK_TOPK):
            m = jnp.max(cand, axis=0, keepdims=True)
            sel = jnp.min(jnp.where(cand == m, cflat, 1 << 20), axis=0, keepdims=True)
            hit = cflat == sel
            fe.append(jnp.max(jnp.where(hit, ce, -1), axis=0, keepdims=True))
            cand = jnp.where(hit, NEG_INF, cand)
            fv.append(m)
        fva = jnp.concatenate(fv, axis=0)
        ex = jnp.exp(fva - fva[0:1])
        r0 = pl.multiple_of(h * PK_TOPK, PK_TOPK)
        gate_t[pl.ds(r0, PK_TOPK), :] = ex / jnp.sum(ex, axis=0, keepdims=True)
        eid_t[pl.ds(r0, PK_TOPK), :] = jnp.concatenate(fe, axis=0).astype(F32)
        return carry

    lax.fori_loop(0, PK_HEADS, head, 0)
    eid_ref[...] = eid_t[...].T.astype(I32)
    gate_ref[...] = gate_t[...].T


def _peer_topk(x, g, wq_heads, sk_pad, *, tt):
    n, d = x.shape
    sel = PK_HEADS * PK_TOPK
    return pl.pallas_call(
        _peer_topk_kernel,
        grid=(n // tt,),
        in_specs=[pl.BlockSpec((tt, d), lambda i: (i, 0)),
                  pl.BlockSpec((1, d), lambda i: (0, 0)),
                  pl.BlockSpec((PK_HEADS, d, LANES), lambda i: (0, 0, 0)),
                  pl.BlockSpec((PK_HEADS, 2 * N_KEYS, LANES), lambda i: (0, 0, 0)),
                  pl.BlockSpec((_CAND_ROWS, tt), lambda i: (0, 0))],
        out_specs=[pl.BlockSpec((tt, d), lambda i: (i, 0)),
                   pl.BlockSpec((tt, sel), lambda i: (i, 0)),
                   pl.BlockSpec((tt, sel), lambda i: (i, 0))],
        out_shape=[jax.ShapeDtypeStruct((n, d), BF16), jax.ShapeDtypeStruct((n, sel), I32),
                   jax.ShapeDtypeStruct((n, sel), F32)],
        scratch_shapes=[pltpu.VMEM((sel, tt), F32), pltpu.VMEM((sel, tt), F32)],
        compiler_params=_cparams(("parallel",)),
        name="peer_topk",
    )(x, g.reshape(1, d), wq_heads, sk_pad, _candidate_tables(tt))


def _peer_dense_kernel(hn_ref, eid_ref, gate_ref, u_ref, v_ref, x_ref, o_ref, w_s, acc_s, *, tn, te):
    j = pl.program_id(1)

    @pl.when(j == 0)
    def _():
        acc_s[...] = jnp.zeros_like(acc_s)
        row_id = lax.broadcasted_iota(I32, (N_KEYS, PK_HEADS * PK_TOPK), 0)

        group = 8

        def build(n8, carry):
            n0 = pl.multiple_of(n8 * group, group)
            e8 = eid_ref[pl.ds(n0, group), :]
            g8 = gate_ref[pl.ds(n0, group), :]
            for r in range(group):
                e = e8[r:r + 1, :]
                g = g8[r:r + 1, :]
                sel_a = jnp.where(row_id == (e >> 7), 1.0, 0.0).astype(BF16)
                hit_b = row_id == (e & (N_KEYS - 1))
                g_hi = g.astype(BF16).astype(F32)
                g_lo = g - g_hi
                w = (lax.dot_general(sel_a, jnp.where(hit_b, g_hi, 0.0).astype(BF16), NT_DIMS,
                                     preferred_element_type=F32)
                     + lax.dot_general(sel_a, jnp.where(hit_b, g_lo, 0.0).astype(BF16), NT_DIMS,
                                       preferred_element_type=F32))
                w_s[pl.ds(pl.multiple_of((n0 + r) * N_KEYS, N_KEYS), N_KEYS), :] = w
            return carry

        lax.fori_loop(0, tn // group, build, 0)

    a = lax.dot_general(hn_ref[...], u_ref[...], NT_DIMS, preferred_element_type=F32)
    pieces = []
    for r in range(te // LANES):
        ar = a[:, r * LANES:(r + 1) * LANES]
        wr = w_s[pl.ds(j * (te // LANES) + r, tn, stride=N_KEYS), :]
        act = 0.5 * ar * (1.0 + lax.erf(ar * (0.5 ** 0.5)))
        pieces.append((wr * act).astype(BF16))
    acc_s[...] += jnp.dot(jnp.concatenate(pieces, axis=1), v_ref[...], preferred_element_type=F32)

    @pl.when(j == pl.num_programs(1) - 1)
    def _():
        o_ref[...] = x_ref[...] + acc_s[...]


def _peer_dense(x, hn, eid, gate, u_all, v_all, *, layer, tn, te):
    n, d = x.shape
    sel = PK_HEADS * PK_TOPK
    row = lambda width: pl.BlockSpec((tn, width), lambda i, j: (i, 0))
    table = pl.BlockSpec((None, te, d), lambda i, j: (layer, j, 0))
    return pl.pallas_call(
        functools.partial(_peer_dense_kernel, tn=tn, te=te),
        grid=(n // tn, N_EXPERTS // te),
        in_specs=[row(d), row(sel), row(sel), table, table, row(d)],
        out_specs=row(d),
        out_shape=jax.ShapeDtypeStruct((n, d), F32),
        scratch_shapes=[pltpu.VMEM((tn * N_KEYS, LANES), F32), pltpu.VMEM((tn, d), F32)],
        compiler_params=_cparams(("parallel", "arbitrary"), vmem_mb=56),
        name="peer_dense",
    )(hn, eid, gate, u_all, v_all, x)


def _pad_tokens(a, tpad):
    return jnp.pad(a, ((0, 0), (0, tpad - a.shape[1]), (0, 0)))


def kernel(x_prompt, x_sample, cache_diff_k, cache_diff_v, cache_mem_k, cache_mem_v, state_hgrn, page_table, mem_prompt, norm_mix_g, w_in, hg_lb_logits, hg_onorm_g, da_qnorm_g, da_knorm_g, da_lambda, da_subln_g, mem_norm_g, w_mem_kv, mem_qnorm_g, mem_knorm_g, w_branch_hg, w_branch_da, w_branch_mem, w_out, norm_ffn_g, peer_wq, peer_subkeys, peer_u, peer_v):
    n_p, seq, d = x_prompt.shape
    n_s, t_new, _ = x_sample.shape
    rows_p = n_p * seq
    rows_s = n_s * t_new
    w = BRANCH_W
    depth = w_in.shape[0]

    w_in_b = w_in.astype(BF16)
    w_kv_b = w_mem_kv.astype(BF16)
    w_hg_b, w_da_b, w_cm_b, w_out_b = (a.astype(BF16) for a in (w_branch_hg, w_branch_da, w_branch_mem, w_out))
    wq_heads = peer_wq.astype(BF16).reshape(depth, d, PK_HEADS, LANES).transpose(0, 2, 1, 3)
    sk = peer_subkeys.astype(BF16)
    zk = jnp.zeros_like(sk[:, :, 0])
    sk_pad = jnp.concatenate([jnp.concatenate([sk[:, :, 0], zk], axis=-1),
                              jnp.concatenate([zk, sk[:, :, 1]], axis=-1)], axis=2)
    u_b = peer_u.astype(BF16)
    v_b = peer_v.astype(BF16)
    cache_k = cache_diff_k.reshape(depth, -1, PAGE_SIZE, w)
    cache_v = cache_diff_v.reshape(depth, -1, PAGE_SIZE, w)
    cmem_k = cache_mem_k.reshape(depth, n_s, MEM_LEN, w)
    cmem_v = cache_mem_v.reshape(depth, n_s, MEM_LEN, w)
    lb_logits = hg_lb_logits.astype(F32)

    mk_p, mv_p = _mem_kv(mem_prompt, mem_norm_g, w_kv_b, mem_knorm_g)

    x = jnp.concatenate([x_prompt.reshape(rows_p, d), x_sample.reshape(rows_s, d)], axis=0)
    row_mask = (np.arange(2 * HEADS * t_new)[:, None] // t_new) == (np.arange(w)[None, :] // DA_DK)
    row_mask = jnp.asarray(row_mask)
    tpad_h = 8
    tpad_c = 16

    hg_p, hg_s, dk_p, dv_p, dk_s, dv_s = [], [], [], [], [], []
    for l in range(depth):
        lam_init = 0.8 - 0.6 * math.exp(-0.3 * l)
        lam_p = da_lambda[l].astype(F32)
        proj = _norm_matmul(x, norm_mix_g[l], w_in_b[l], tn=512, tc=1024)
        qn, kn, knb, vb, qm = _prep(proj, da_qnorm_g[l], da_knorm_g[l], mem_qnorm_g[l], tn=512)

        hgo_p, st_p = _hgrn(proj, lb_logits, hg_onorm_g[l], None, layer=l, n_seq=n_p, seq_len=seq,
                            row0=0, nb=1, tc=256, c=16, tvalid=16, out_dtype=BF16)
        src_s = _pad_tokens(proj[rows_p:, :4 * w].reshape(n_s, t_new, 4 * w), tpad_h).reshape(n_s * tpad_h, 4 * w)
        hgo_s, st_s = _hgrn(src_s, lb_logits, hg_onorm_g[l], state_hgrn, layer=l, n_seq=n_s, seq_len=tpad_h,
                            row0=0, nb=8, tc=tpad_h, c=tpad_h, tvalid=t_new, out_dtype=F32)
        hgo_s = hgo_s.reshape(n_s, tpad_h, w)[:, :t_new].reshape(rows_s, w).astype(BF16)

        da_p = _dattn_prompt(qn, knb, vb, lam_p, da_subln_g[l], n_seq=n_p, seq_len=seq, tq=512,
                             lam_init=lam_init)
        qs = qn[rows_p:].reshape(n_s, t_new, w)
        qrows = jnp.where(row_mask, jnp.tile(qs, (1, 2 * HEADS, 1)), jnp.zeros((), BF16))
        da_s = _dattn_sample(page_table, qrows, knb[rows_p:].reshape(n_s, t_new, w).astype(F32),
                             vb[rows_p:].reshape(n_s, t_new, w).astype(F32), cache_k, cache_v, lam_p, da_subln_g[l],
                             layer=l, lam_init=lam_init)
        da_s = da_s.reshape(rows_s, w).astype(BF16)

        cm_p = _cross_prompt(qm, mk_p, mv_p, layer=l, n_seq=n_p, seq_len=seq, tq=512)
        qm_s = _pad_tokens(qm[rows_p:].reshape(n_s, t_new, w), tpad_c)
        cm_s = _cross_sample(qm_s, cmem_k, cmem_v, layer=l)[:, :t_new].reshape(rows_s, w).astype(BF16)

        x = _merge(x, proj, jnp.concatenate([hgo_p, hgo_s]), jnp.concatenate([da_p, da_s]),
                   jnp.concatenate([cm_p, cm_s]), w_hg_b[l], w_da_b[l], w_cm_b[l], w_out_b[l], tn=512)

        hn, eid, gate = _peer_topk(x, norm_ffn_g[l], wq_heads[l], sk_pad[l], tt=256)
        x = _peer_dense(x, hn, eid, gate, u_b, v_b, layer=l, tn=256, te=1024)

        hg_p.append(st_p)
        hg_s.append(st_s)
        dk_p.append(kn[:rows_p].reshape(n_p, seq, HEADS, 2, DA_DK))
        dv_p.append(proj[:rows_p, 6 * w:7 * w].reshape(n_p, seq, HEADS, HEAD_W))
        dk_s.append(kn[rows_p:].reshape(n_s, t_new, HEADS, 2, DA_DK))
        dv_s.append(proj[rows_p:, 6 * w:7 * w].reshape(n_s, t_new, HEADS, HEAD_W))

    mem_shape = (depth, n_p, MEM_LEN, HEADS, HEAD_W)
    return (x[:rows_p].reshape(n_p, seq, d), x[rows_p:].reshape(n_s, t_new, d),
            jnp.stack(hg_p), jnp.stack(hg_s), jnp.stack(dk_p), jnp.stack(dv_p),
            jnp.stack(dk_s), jnp.stack(dv_s), mk_p.reshape(mem_shape), mv_p.reshape(mem_shape))
```

```python
import functools
import math

import numpy as np
import jax
import jax.numpy as jnp
from jax import lax
from jax.experimental import pallas as pl
from jax.experimental.pallas import tpu as pltpu

F32 = jnp.float32
BF16 = jnp.bfloat16
I32 = jnp.int32

D_MODEL = 1024
DEPTH = 4
PAGE_SIZE = 128
HEADS = 4
HEAD_W = 128
BRANCH_W = HEADS * HEAD_W
DA_DK = 64
MEM_LEN = 256
N_BRANCH = 3
IN_COLS = 8 * BRANCH_W + N_BRANCH * D_MODEL
PK_HEADS = 8
N_KEYS = 128
N_EXPERTS = N_KEYS * N_KEYS
PK_TOPK = 16
EPS = 1e-6
MASK_VALUE = -1e30
NEG_INF = float("-inf")

LANES = 128

ROW_TILE = 512
PROJ_COL_TILE = 1024
ATTN_TILE = 512
ATTN_ROW_CHUNK = 256
HGRN_BLOCK = 256
HGRN_CHUNK = 16
PEER_TOPK_TILE = 512
PEER_TOKEN_TILE = 256
PEER_EXPERT_TILE = 1024
NT_DIMS = (((1,), (1,)), ((), ()))
TN_DIMS = (((0,), (0,)), ((), ()))

_CAND_ROWS = 80
_ID_SENTINEL = float(1 << 20)


def _cparams(sem, vmem_mb=None):
    kw = dict(dimension_semantics=sem)
    if vmem_mb is not None:
        kw["vmem_limit_bytes"] = vmem_mb * 2 ** 20
    return pltpu.CompilerParams(**kw)


def _split_dot(x, m_bf16, terms, left=False):
    acc = None
    r = x
    for _ in range(terms):
        p = r.astype(BF16)
        r = r - p.astype(F32)
        if left:
            y = jnp.dot(m_bf16, p, preferred_element_type=F32)
        else:
            y = jnp.dot(p, m_bf16, preferred_element_type=F32)
        acc = y if acc is None else acc + y
    return acc


def _sigmoid(x):
    return 1.0 / (1.0 + jnp.exp(-x))


def _norm_matmul_kernel(x_ref, g_ref, w_ref, o_ref, hn_ref):
    @pl.when(pl.program_id(1) == 0)
    def _():
        x = x_ref[...]
        ms = jnp.mean(x * x, axis=-1, keepdims=True)
        hn_ref[...] = ((x * lax.rsqrt(ms + EPS)) * g_ref[...]).astype(BF16)

    o_ref[...] = jnp.dot(hn_ref[...], w_ref[...], preferred_element_type=F32)


def _norm_matmul(x, g, w, *, tn, tc):
    n, d = x.shape
    c = w.shape[1]
    return pl.pallas_call(
        _norm_matmul_kernel,
        grid=(n // tn, c // tc),
        in_specs=[
            pl.BlockSpec((tn, d), lambda i, j: (i, 0)),
            pl.BlockSpec((1, d), lambda i, j: (0, 0)),
            pl.BlockSpec((d, tc), lambda i, j: (0, j)),
        ],
        out_specs=pl.BlockSpec((tn, tc), lambda i, j: (i, j)),
        out_shape=jax.ShapeDtypeStruct((n, c), F32),
        scratch_shapes=[pltpu.VMEM((tn, d), BF16)],
        compiler_params=_cparams(("parallel", "arbitrary")),
        name="norm_matmul",
    )(x, g.reshape(1, d), w)


def _group_mean_matrix(width, group):
    r = np.arange(width)
    return jnp.asarray((r[:, None] // group == r[None, :] // group) / group, BF16)


def _prep_kernel(dq_ref, dk_ref, dv_ref, mq_ref, gq_ref, gk_ref, gm_ref, bd64_ref, bd128_ref,
                 qn_ref, kn_ref, knb_ref, vb_ref, qm_ref):
    def gnorm(x, bd, g):
        ms = _split_dot(x * x, bd, 3)
        return (x * lax.rsqrt(ms + EPS)) * g

    q = gnorm(dq_ref[...], bd64_ref[...], gq_ref[...])
    qn_ref[...] = (q * (DA_DK ** -0.5)).astype(BF16)
    k = gnorm(dk_ref[...], bd64_ref[...], gk_ref[...])
    kn_ref[...] = k
    knb_ref[...] = k.astype(BF16)
    vb_ref[...] = dv_ref[...].astype(BF16)
    qm_ref[...] = gnorm(mq_ref[...], bd128_ref[...], gm_ref[...]).astype(BF16)


def _prep(proj, gq, gk, gm, *, tn):
    n = proj.shape[0]
    w = BRANCH_W
    col = lambda c: pl.BlockSpec((tn, w), lambda i, c=c: (i, c))
    const = lambda shape: pl.BlockSpec(shape, lambda i: (0, 0))
    row_out = pl.BlockSpec((tn, w), lambda i: (i, 0))
    tile = lambda g: jnp.tile(g.astype(F32), w // g.shape[0]).reshape(1, w)
    return pl.pallas_call(
        _prep_kernel,
        grid=(n // tn,),
        in_specs=[col(4), col(5), col(6), col(7), const((1, w)), const((1, w)), const((1, w)),
                  const((w, w)), const((w, w))],
        out_specs=[row_out] * 5,
        out_shape=[jax.ShapeDtypeStruct((n, w), BF16), jax.ShapeDtypeStruct((n, w), F32),
                   jax.ShapeDtypeStruct((n, w), BF16), jax.ShapeDtypeStruct((n, w), BF16),
                   jax.ShapeDtypeStruct((n, w), BF16)],
        compiler_params=_cparams(("parallel",)),
        name="qk_norm",
    )(proj, proj, proj, proj, tile(gq), tile(gk), tile(gm),
      _group_mean_matrix(w, DA_DK), _group_mean_matrix(w, HEAD_W))


def _hgrn_kernel(*refs, layer, nb, tc, c, tvalid, has_s0):
    if has_s0:
        hq_ref, hf_ref, hi_ref, hg_ref, lbl_ref, og_ref, tri_ref, s0_ref = refs[:8]
        rest = refs[8:]
    else:
        hq_ref, hf_ref, hi_ref, hg_ref, lbl_ref, og_ref, tri_ref = refs[:7]
        s0_ref = None
        rest = refs[7:]
    o_ref, sout_ref, st_ref, q_s, k_s, b_s = rest
    t = pl.program_id(1)
    rows = nb * tc

    @pl.when(t == 0)
    def _():
        if has_s0:
            for ib in range(nb):
                for h in range(HEADS):
                    st_ref[ib, h] = s0_ref[ib, h].T
        else:
            st_ref[...] = jnp.zeros_like(st_ref)

    lg = lbl_ref[...]
    e = jnp.exp(lg - jnp.max(lg, axis=0, keepdims=True))
    pr = e / jnp.sum(e, axis=0, keepdims=True)
    lb = jnp.zeros((1, BRANCH_W), F32)
    for i in range(1, layer + 1):
        lb = lb + pr[i:i + 1]
    lb = jnp.clip(lb, 0.0, 1.0 - 1e-4)

    z = hf_ref[...]
    sg = _sigmoid(z)
    logf = jnp.log(lb + (1.0 - lb) * sg)
    kk = (1.0 - lb) * _sigmoid(-z)
    if tvalid < c:
        valid = (lax.broadcasted_iota(I32, (rows, BRANCH_W), 0) % c) < tvalid
        logf = jnp.where(valid, logf, 0.0)
        kk = jnp.where(valid, kk, 0.0)
    hq = hq_ref[...]
    q_s[...] = hq * _sigmoid(hq)
    k_s[...] = kk
    b_s[...] = _split_dot(logf, tri_ref[...], 3, left=True)

    og = og_ref[...]
    row_id = lax.broadcasted_iota(I32, (c, HEAD_W), 0)
    nsub = tc // c

    def sub(i, carry):
        r0 = pl.multiple_of(i * c, c)
        ib = i // nsub
        first = (i % nsub) == 0
        new_carry = []
        for h in range(HEADS):
            sl = slice(h * HEAD_W, (h + 1) * HEAD_W)
            bg = b_s[pl.ds(r0, c), sl]
            new_carry.append(bg[c - 1:c, :])
            b = bg - jnp.where(first, 0.0, carry[h])
            q = q_s[pl.ds(r0, c), sl]
            kx = k_s[pl.ds(r0, c), sl]
            iv = hi_ref[pl.ds(r0, c), sl]
            st = st_ref[ib, h]
            o_inter = lax.dot_general((q * jnp.exp(b)).astype(BF16), st.astype(BF16), NT_DIMS,
                                      preferred_element_type=F32)
            o_rows = []
            for tt in range(c):
                d = b[tt:tt + 1, :] - b
                w = jnp.exp(jnp.where(row_id <= tt, d, MASK_VALUE))
                p = (q[tt:tt + 1, :] * kx) * w
                a = jnp.sum(p, axis=-1, keepdims=True)
                o_rows.append(jnp.sum(a * iv, axis=0, keepdims=True))
            o = o_inter + jnp.concatenate(o_rows, axis=0)
            bl = b[c - 1:c, :]
            kd = kx * jnp.exp(bl - b)
            upd = lax.dot_general(iv.astype(BF16), kd.astype(BF16), TN_DIMS, preferred_element_type=F32)
            st_ref[ib, h] = jnp.exp(bl) * st + upd
            ms = jnp.mean(o * o, axis=-1, keepdims=True)
            hg = hg_ref[pl.ds(r0, c), sl]
            o_ref[pl.ds(r0, c), sl] = (((o * lax.rsqrt(ms + EPS)) * og) * (hg * _sigmoid(hg))).astype(o_ref.dtype)
        return tuple(new_carry)

    lax.fori_loop(0, nb * nsub, sub, tuple(jnp.zeros((1, HEAD_W), F32) for _ in range(HEADS)))

    @pl.when(t == pl.num_programs(1) - 1)
    def _():
        for ib in range(nb):
            for h in range(HEADS):
                sout_ref[ib, h] = st_ref[ib, h].T


def _hgrn(src, lb_logits, onorm_g, s0_all, *, layer, n_seq, seq_len, rows_out, row0, nb, tc, c, tvalid,
          out_dtype):
    rows = nb * tc
    nt = seq_len // tc
    rb0 = row0 // rows
    w = BRANCH_W
    col = lambda cc: pl.BlockSpec((rows, w), lambda b, t, cc=cc: (rb0 + b * nt + t, cc))
    r = np.arange(rows)
    tri = jnp.asarray((r[:, None] >= r[None, :]) & (r[:, None] // tc == r[None, :] // tc), BF16)
    in_specs = [col(0), col(1), col(2), col(3),
                pl.BlockSpec((DEPTH, w), lambda b, t: (0, 0)),
                pl.BlockSpec((1, HEAD_W), lambda b, t: (0, 0)),
                pl.BlockSpec((rows, rows), lambda b, t: (0, 0))]
    args = [src, src, src, src, lb_logits, onorm_g.reshape(1, HEAD_W), tri]
    has_s0 = s0_all is not None
    if has_s0:
        in_specs.append(pl.BlockSpec((None, nb, HEADS, HEAD_W, HEAD_W), lambda b, t: (layer, b, 0, 0, 0)))
        args.append(s0_all)
    kern = functools.partial(_hgrn_kernel, layer=layer, nb=nb, tc=tc, c=c, tvalid=tvalid, has_s0=has_s0)
    return pl.pallas_call(
        kern,
        grid=(n_seq // nb, nt),
        in_specs=in_specs,
        out_specs=[pl.BlockSpec((rows, w), lambda b, t: (b * nt + t, 0)),
                   pl.BlockSpec((nb, HEADS, HEAD_W, HEAD_W), lambda b, t: (b, 0, 0, 0))],
        out_shape=[jax.ShapeDtypeStruct((rows_out, w), out_dtype),
                   jax.ShapeDtypeStruct((n_seq, HEADS, HEAD_W, HEAD_W), F32)],
        scratch_shapes=[pltpu.VMEM((nb, HEADS, HEAD_W, HEAD_W), F32),
                        pltpu.VMEM((rows, w), F32), pltpu.VMEM((rows, w), F32), pltpu.VMEM((rows, w), F32)],
        compiler_params=_cparams(("parallel", "arbitrary")),
        name="hgrn2",
    )(*args)


def _lambda_value(lp, lam_init):
    a = jnp.sum(lp[0:1] * lp[1:2], axis=-1, keepdims=True)
    b = jnp.sum(lp[2:3] * lp[3:4], axis=-1, keepdims=True)
    return jnp.exp(a) - jnp.exp(b) + lam_init


def _dattn_prompt_kernel(iq_tab, ik_tab, lam_ref, q_ref, k_ref, v_ref, sg_ref, o_ref, q2_s, m_s, l_s, acc_s,
                         *, tq, rc, lam_init):
    step = pl.program_id(2)
    iq = iq_tab[step]
    ik = ik_tab[step]

    @pl.when(ik == 0)
    def _():
        q = q_ref[...].astype(F32)
        lane = lax.broadcasted_iota(I32, q.shape, 1)
        q2_s[0:tq, :] = jnp.where(lane < DA_DK, q, 0.0).astype(BF16)
        q2_s[tq:2 * tq, :] = jnp.where(lane >= DA_DK, q, 0.0).astype(BF16)
        m_s[...] = jnp.full_like(m_s, NEG_INF)
        l_s[...] = jnp.zeros_like(l_s)
        acc_s[...] = jnp.zeros_like(acc_s)

    def update(masked):
        k = k_ref[...]
        v = v_ref[...]
        for c0 in range(0, 2 * tq, rc):
            rows = slice(c0, c0 + rc)
            s = lax.dot_general(q2_s[rows, :], k, NT_DIMS, preferred_element_type=F32)
            if masked:
                r = lax.broadcasted_iota(I32, s.shape, 0) + (c0 % tq)
                cidx = lax.broadcasted_iota(I32, s.shape, 1)
                s = jnp.where(cidx <= r, s, MASK_VALUE)
            m_prev = m_s[rows, :]
            m_new = jnp.maximum(m_prev, jnp.max(s, axis=-1, keepdims=True))
            alpha = jnp.exp(m_prev - m_new)
            p = jnp.exp(s - jnp.tile(m_new, (1, tq // LANES)))
            l_s[rows, :] = alpha * l_s[rows, :] + jnp.sum(p, axis=-1, keepdims=True)
            acc_s[rows, :] = alpha * acc_s[rows, :] + jnp.dot(p.astype(BF16), v, preferred_element_type=F32)
            m_s[rows, :] = m_new

    @pl.when(ik < iq)
    def _():
        update(False)

    @pl.when(ik == iq)
    def _():
        update(True)
        lam = _lambda_value(lam_ref[...], lam_init)
        o0 = acc_s[0:tq, :] / l_s[0:tq, :]
        o1 = acc_s[tq:2 * tq, :] / l_s[tq:2 * tq, :]
        da = o0 - lam * o1
        ms = jnp.mean(da * da, axis=-1, keepdims=True)
        o_ref[...] = (((da * lax.rsqrt(ms + EPS)) * sg_ref[...]) * (1.0 - lam_init)).astype(o_ref.dtype)


def _dattn_prompt(qn, knb, vb, lam_p, subln_g, *, n_seq, seq_len, rows_out, tq, rc, lam_init):
    nq = seq_len // tq
    pairs = [(iq, ik) for iq in range(nq) for ik in range(iq + 1)]
    iq_tab = jnp.asarray([p[0] for p in pairs], I32)
    ik_tab = jnp.asarray([p[1] for p in pairs], I32)
    qspec = pl.BlockSpec((tq, HEAD_W), lambda b, h, s, iqt, ikt: (b * nq + iqt[s], h))
    kspec = pl.BlockSpec((tq, HEAD_W), lambda b, h, s, iqt, ikt: (b * nq + ikt[s], h))
    grid_spec = pltpu.PrefetchScalarGridSpec(
        num_scalar_prefetch=2,
        grid=(n_seq, HEADS, len(pairs)),
        in_specs=[pl.BlockSpec((4, DA_DK), lambda b, h, s, iqt, ikt: (0, 0)), qspec, kspec, kspec,
                  pl.BlockSpec((1, HEAD_W), lambda b, h, s, iqt, ikt: (0, 0))],
        out_specs=qspec,
        scratch_shapes=[pltpu.VMEM((2 * tq, HEAD_W), BF16), pltpu.VMEM((2 * tq, LANES), F32),
                        pltpu.VMEM((2 * tq, LANES), F32), pltpu.VMEM((2 * tq, HEAD_W), F32)],
    )
    return pl.pallas_call(
        functools.partial(_dattn_prompt_kernel, tq=tq, rc=rc, lam_init=lam_init),
        grid_spec=grid_spec,
        out_shape=jax.ShapeDtypeStruct((rows_out, BRANCH_W), BF16),
        compiler_params=_cparams(("parallel", "parallel", "arbitrary")),
        name="diff_attn_prompt",
    )(iq_tab, ik_tab, lam_p, qn, knb, vb, subln_g.reshape(1, HEAD_W))


def _dattn_sample_kernel(pt_ref, lam_ref, qr_ref, kc_ref, vc_ref, sg_ref, *rest, n_pages, t_new, lam_init):
    k_refs = rest[:n_pages]
    v_refs = rest[n_pages:2 * n_pages]
    o_ref = rest[2 * n_pages]
    nrow = 2 * HEADS * t_new
    qr = qr_ref[...]
    s_past = jnp.concatenate(
        [jnp.dot(qr, k[...].astype(BF16), preferred_element_type=F32) for k in k_refs], axis=1)

    qf = qr.astype(F32)
    kc = kc_ref[...]
    vc = vc_ref[...]
    t_of_row = lax.broadcasted_iota(I32, (nrow, 1), 0) % t_new
    s_new = []
    for j in range(t_new):
        sj = jnp.sum(qf * kc[j:j + 1, :], axis=-1, keepdims=True)
        s_new.append(jnp.where(t_of_row >= j, sj, MASK_VALUE))
    m = jnp.max(s_past, axis=-1, keepdims=True)
    for sj in s_new:
        m = jnp.maximum(m, sj)
    p_past = jnp.exp(s_past - m)
    p_new = [jnp.exp(sj - m) for sj in s_new]
    l = jnp.sum(p_past, axis=-1, keepdims=True)
    for pj in p_new:
        l = l + pj

    lam = _lambda_value(lam_ref[...], lam_init)
    rows_h = 2 * t_new
    for h in range(HEADS):
        sl = slice(h * HEAD_W, (h + 1) * HEAD_W)
        r0 = h * rows_h
        v_h = jnp.concatenate([v[pl.ds(h, PAGE_SIZE, stride=HEADS), :] for v in v_refs], axis=0)
        acc = jnp.dot(p_past[r0:r0 + rows_h].astype(BF16), v_h.astype(BF16), preferred_element_type=F32)
        for j in range(t_new):
            acc = acc + p_new[j][r0:r0 + rows_h] * vc[j:j + 1, sl]
        on = acc / l[r0:r0 + rows_h]
        da = on[:t_new] - lam * on[t_new:]
        ms = jnp.mean(da * da, axis=-1, keepdims=True)
        o_ref[:, sl] = (((da * lax.rsqrt(ms + EPS)) * sg_ref[...]) * (1.0 - lam_init)).astype(o_ref.dtype)


def _dattn_sample(page_table, qrows, kcur, vcur, cache_kt, cache_v, lam_p, subln_g, *, layer, lam_init):
    n_seq, n_pages = page_table.shape
    t_new = kcur.shape[1]
    nrow = qrows.shape[1]
    per_seq = lambda shape: pl.BlockSpec((None,) + shape, lambda b, pt: (b, 0, 0))
    page = lambda p: pl.BlockSpec((None, None, PAGE_SIZE * HEADS, HEAD_W),
                                  lambda b, pt, p=p: (layer, pt[b * n_pages + p], 0, 0))
    grid_spec = pltpu.PrefetchScalarGridSpec(
        num_scalar_prefetch=1,
        grid=(n_seq,),
        in_specs=[pl.BlockSpec((4, DA_DK), lambda b, pt: (0, 0)),
                  per_seq((nrow, BRANCH_W)), per_seq((t_new, BRANCH_W)), per_seq((t_new, BRANCH_W)),
                  pl.BlockSpec((1, HEAD_W), lambda b, pt: (0, 0))]
                 + [page(p) for p in range(n_pages)] * 2,
        out_specs=per_seq((t_new, BRANCH_W)),
    )
    return pl.pallas_call(
        functools.partial(_dattn_sample_kernel, n_pages=n_pages, t_new=t_new, lam_init=lam_init),
        grid_spec=grid_spec,
        out_shape=jax.ShapeDtypeStruct((n_seq, t_new, BRANCH_W), F32),
        compiler_params=_cparams(("parallel",), vmem_mb=40),
        name="diff_attn_sample",
    )(page_table.reshape(-1), lam_p, qrows, kcur, vcur, subln_g.reshape(1, HEAD_W),
      *([cache_kt] * n_pages), *([cache_v] * n_pages))


def _mem_kv_kernel(m_ref, g_ref, w_ref, kg_ref, bd_ref, mk_ref, mv_ref):
    x = m_ref[...]
    ms = jnp.mean(x * x, axis=-1, keepdims=True)
    hn = ((x * lax.rsqrt(ms + EPS)) * g_ref[...]).astype(BF16)
    kv = jnp.dot(hn, w_ref[...], preferred_element_type=F32)
    k = kv[:, :BRANCH_W]
    ms = _split_dot(k * k, bd_ref[...], 3)
    mk_ref[...] = (k * lax.rsqrt(ms + EPS)) * kg_ref[...]
    mv_ref[...] = kv[:, BRANCH_W:]


def _mem_kv(mem, norm_g, w_kv, knorm_g):
    n_seq = mem.shape[0]
    w = BRANCH_W
    out = pl.BlockSpec((None, None, MEM_LEN, w), lambda l, b: (l, b, 0, 0))
    return pl.pallas_call(
        _mem_kv_kernel,
        grid=(DEPTH, n_seq),
        in_specs=[pl.BlockSpec((None, MEM_LEN, D_MODEL), lambda l, b: (b, 0, 0)),
                  pl.BlockSpec((None, 1, D_MODEL), lambda l, b: (l, 0, 0)),
                  pl.BlockSpec((None, D_MODEL, 2 * w), lambda l, b: (l, 0, 0)),
                  pl.BlockSpec((None, 1, w), lambda l, b: (l, 0, 0)),
                  pl.BlockSpec((w, w), lambda l, b: (0, 0))],
        out_specs=[out, out],
        out_shape=[jax.ShapeDtypeStruct((DEPTH, n_seq, MEM_LEN, w), F32)] * 2,
        compiler_params=_cparams(("parallel", "parallel")),
        name="mem_kv",
    )(mem, norm_g.reshape(DEPTH, 1, D_MODEL), w_kv,
      jnp.tile(knorm_g.astype(F32), (1, HEADS)).reshape(DEPTH, 1, w), _group_mean_matrix(w, HEAD_W))


def _cross_kernel(q_ref, mk_ref, mv_ref, o_ref, *, head_rows):
    q = q_ref[...]
    for h in range(HEADS):
        sl = slice(h * HEAD_W, (h + 1) * HEAD_W)
        if head_rows:
            mk = mk_ref[pl.ds(h, MEM_LEN, stride=HEADS), :].astype(BF16)
            mv = mv_ref[pl.ds(h, MEM_LEN, stride=HEADS), :].astype(BF16)
        else:
            mk = mk_ref[:, sl].astype(BF16)
            mv = mv_ref[:, sl].astype(BF16)
        s = lax.dot_general(q[:, sl], mk, NT_DIMS, preferred_element_type=F32) * (HEAD_W ** -0.5)
        p = jnp.exp(s - jnp.max(s, axis=-1, keepdims=True))
        l = jnp.sum(p, axis=-1, keepdims=True)
        o = jnp.dot(p.astype(BF16), mv, preferred_element_type=F32) / l
        o_ref[:, sl] = o.astype(o_ref.dtype)


def _cross_prompt(qm, mk_all, mv_all, *, layer, n_seq, seq_len, rows_out, tq):
    nq = seq_len // tq
    mem = pl.BlockSpec((None, None, MEM_LEN, BRANCH_W), lambda b, i: (layer, b, 0, 0))
    row = pl.BlockSpec((tq, BRANCH_W), lambda b, i: (b * nq + i, 0))
    return pl.pallas_call(
        functools.partial(_cross_kernel, head_rows=False),
        grid=(n_seq, nq),
        in_specs=[row, mem, mem],
        out_specs=row,
        out_shape=jax.ShapeDtypeStruct((rows_out, BRANCH_W), BF16),
        compiler_params=_cparams(("parallel", "parallel")),
        name="cross_attn_prompt",
    )(qm, mk_all, mv_all)


def _cross_sample(qm_pad, mk_all, mv_all, *, layer):
    n_seq, tpad, w = qm_pad.shape
    mem = pl.BlockSpec((None, None, MEM_LEN * HEADS, HEAD_W), lambda b: (layer, b, 0, 0))
    row = pl.BlockSpec((None, tpad, w), lambda b: (b, 0, 0))
    return pl.pallas_call(
        functools.partial(_cross_kernel, head_rows=True),
        grid=(n_seq,),
        in_specs=[row, mem, mem],
        out_specs=row,
        out_shape=jax.ShapeDtypeStruct((n_seq, tpad, w), F32),
        compiler_params=_cparams(("parallel",)),
        name="cross_attn_sample",
    )(qm_pad, mk_all, mv_all)


def _merge_kernel(x_ref, g0_ref, g1_ref, g2_ref, a0_ref, a1_ref, a2_ref, w0_ref, w1_ref, w2_ref, wo_ref, o_ref):
    merged = None
    for g_ref, a_ref, w_ref in ((g0_ref, a0_ref, w0_ref), (g1_ref, a1_ref, w1_ref), (g2_ref, a2_ref, w2_ref)):
        y = _sigmoid(g_ref[...]) * jnp.dot(a_ref[...], w_ref[...], preferred_element_type=F32)
        merged = y if merged is None else merged + y
    o_ref[...] = x_ref[...] + jnp.dot(merged.astype(BF16), wo_ref[...], preferred_element_type=F32)


def _merge(x, proj, hg, da, cm, w_hg, w_da, w_mem, w_out, *, tn):
    n, d = x.shape
    w = BRANCH_W
    gate_col0 = (8 * w) // d
    row = lambda width: pl.BlockSpec((tn, width), lambda i: (i, 0))
    gate = lambda c: pl.BlockSpec((tn, d), lambda i, c=c: (i, gate_col0 + c))
    const = lambda shape: pl.BlockSpec(shape, lambda i: (0, 0))
    return pl.pallas_call(
        _merge_kernel,
        grid=(n // tn,),
        in_specs=[row(d), gate(0), gate(1), gate(2), row(w), row(w), row(w),
                  const((w, d)), const((w, d)), const((w, d)), const((d, d))],
        out_specs=row(d),
        out_shape=jax.ShapeDtypeStruct((n, d), F32),
        compiler_params=_cparams(("parallel",), vmem_mb=48),
        name="branch_merge",
    )(x, proj, proj, proj, hg, da, cm, w_hg, w_da, w_mem, w_out)


def _topk_rows(s, k, key_id):
    vals, ids = [], []
    for _ in range(k):
        m = jnp.max(s, axis=0, keepdims=True)
        sel = jnp.min(jnp.where(s == m, key_id, _ID_SENTINEL), axis=0, keepdims=True)
        s = jnp.where(key_id == sel, NEG_INF, s)
        vals.append(m)
        ids.append(sel)
    return vals, ids


def _candidate_tables(tt):
    flat = np.full((_CAND_ROWS,), _ID_SENTINEL, np.float64)
    flat[0:16] = np.arange(16)
    flat[16:24] = 16 + np.arange(8)
    for j1 in range(2, 8):
        lim = PK_TOPK // (j1 + 1)
        flat[24 + (j1 - 2) * 8:24 + (j1 - 2) * 8 + lim] = j1 * 16 + np.arange(lim)
    flat[72:80] = (8 + np.arange(8)) * 16
    return jnp.asarray(np.broadcast_to(flat[:, None], (_CAND_ROWS, tt)), F32)


def _peer_topk_kernel(x_ref, g_ref, wq_ref, sk_ref, cflat_ref, hn_ref, eid_ref, gate_ref, eid_t, gate_t):
    x = x_ref[...]
    ms = jnp.mean(x * x, axis=-1, keepdims=True)
    hn = ((x * lax.rsqrt(ms + EPS)) * g_ref[...]).astype(BF16)
    hn_ref[...] = hn
    tt = x.shape[0]
    key_id = lax.broadcasted_iota(I32, (N_KEYS, tt), 0).astype(F32)
    cflat = cflat_ref[...]
    cvalid = cflat < float(PK_TOPK * PK_TOPK)

    def head(h, carry):
        qh = jnp.dot(hn, wq_ref[h], preferred_element_type=F32).astype(BF16)
        sc = lax.dot_general(sk_ref[h], qh, NT_DIMS, preferred_element_type=F32)
        v1, i1 = _topk_rows(sc[:N_KEYS], PK_TOPK, key_id)
        v2, i2 = _topk_rows(sc[N_KEYS:], PK_TOPK, key_id)
        v2a = jnp.concatenate(v2, axis=0)
        i2a = jnp.concatenate(i2, axis=0)
        pieces_v = [v1[0] + v2a, v1[1] + v2a[:8]]
        pieces_e = [i1[0] * N_KEYS + i2a, i1[1] * N_KEYS + i2a[:8]]
        for j1 in range(2, 8):
            pieces_v.append(v1[j1] + v2a[:8])
            pieces_e.append(i1[j1] * N_KEYS + i2a[:8])
        pieces_v.append(jnp.concatenate(v1[8:], axis=0) + v2a[:1])
        pieces_e.append(jnp.concatenate(i1[8:], axis=0) * N_KEYS + i2a[:1])
        cand = jnp.where(cvalid, jnp.concatenate(pieces_v, axis=0), NEG_INF)
        ce = jnp.concatenate(pieces_e, axis=0)
        fv, fe = [], []
        for _ in range(PK_TOPK):
            m = jnp.max(cand, axis=0, keepdims=True)
            sel = jnp.min(jnp.where(cand == m, cflat, _ID_SENTINEL), axis=0, keepdims=True)
            hit = cflat == sel
            fe.append(jnp.max(jnp.where(hit, ce, -1.0), axis=0, keepdims=True))
            cand = jnp.where(hit, NEG_INF, cand)
            fv.append(m)
        fva = jnp.concatenate(fv, axis=0)
        ex = jnp.exp(fva - fva[0:1])
        r0 = pl.multiple_of(h * PK_TOPK, PK_TOPK)
        gate_t[pl.ds(r0, PK_TOPK), :] = ex / jnp.sum(ex, axis=0, keepdims=True)
        eid_t[pl.ds(r0, PK_TOPK), :] = jnp.concatenate(fe, axis=0)
        return carry

    lax.fori_loop(0, PK_HEADS, head, 0)
    eid_ref[...] = eid_t[...].T.astype(I32)
    gate_ref[...] = gate_t[...].T


def _peer_topk(x, g, wq_heads, sk_pad, *, tt):
    n, d = x.shape
    sel = PK_HEADS * PK_TOPK
    return pl.pallas_call(
        _peer_topk_kernel,
        grid=(n // tt,),
        in_specs=[pl.BlockSpec((tt, d), lambda i: (i, 0)),
                  pl.BlockSpec((1, d), lambda i: (0, 0)),
                  pl.BlockSpec((PK_HEADS, d, LANES), lambda i: (0, 0, 0)),
                  pl.BlockSpec((PK_HEADS, 2 * N_KEYS, LANES), lambda i: (0, 0, 0)),
                  pl.BlockSpec((_CAND_ROWS, tt), lambda i: (0, 0))],
        out_specs=[pl.BlockSpec((tt, d), lambda i: (i, 0)),
                   pl.BlockSpec((tt, sel), lambda i: (i, 0)),
                   pl.BlockSpec((tt, sel), lambda i: (i, 0))],
        out_shape=[jax.ShapeDtypeStruct((n, d), BF16), jax.ShapeDtypeStruct((n, sel), I32),
                   jax.ShapeDtypeStruct((n, sel), F32)],
        scratch_shapes=[pltpu.VMEM((sel, tt), F32), pltpu.VMEM((sel, tt), F32)],
        compiler_params=_cparams(("parallel",)),
        name="peer_topk",
    )(x, g.reshape(1, d), wq_heads, sk_pad, _candidate_tables(tt))


def _peer_dense_kernel(hn_ref, eid_ref, gate_ref, u_ref, v_ref, x_ref, o_ref, w_s, acc_s, *, tn, te, pitch):
    j = pl.program_id(1)

    @pl.when(j == 0)
    def _():
        acc_s[...] = jnp.zeros_like(acc_s)
        row_id = lax.broadcasted_iota(I32, (N_KEYS, PK_HEADS * PK_TOPK), 0)

        group = 8

        def build(n8, carry):
            n0 = pl.multiple_of(n8 * group, group)
            e8 = eid_ref[pl.ds(n0, group), :]
            g8 = gate_ref[pl.ds(n0, group), :]
            for r in range(group):
                e = e8[r:r + 1, :]
                g = g8[r:r + 1, :]
                sel_a = jnp.where(row_id == (e >> 7), 1.0, 0.0).astype(BF16)
                hit_b = row_id == (e & (N_KEYS - 1))
                g_hi = g.astype(BF16).astype(F32)
                g_lo = g - g_hi
                lhs = jnp.concatenate([sel_a, sel_a], axis=1)
                rhs = jnp.concatenate([jnp.where(hit_b, g_hi, 0.0).astype(BF16),
                                       jnp.where(hit_b, g_lo, 0.0).astype(BF16)], axis=1)
                w = lax.dot_general(lhs, rhs, NT_DIMS, preferred_element_type=F32)
                w_s[pl.ds(n0 + r, N_KEYS, stride=pitch), :] = w
            return carry

        lax.fori_loop(0, tn // group, build, 0)

    a = lax.dot_general(hn_ref[...], u_ref[...], NT_DIMS, preferred_element_type=F32)
    pieces = []
    for r in range(te // LANES):
        ar = a[:, r * LANES:(r + 1) * LANES]
        wr = w_s[pl.ds(pl.multiple_of((j * (te // LANES) + r) * pitch, 8), tn), :]
        act = 0.5 * ar * (1.0 + lax.erf(ar * (0.5 ** 0.5)))
        pieces.append((wr * act).astype(BF16))
    acc_s[...] += jnp.dot(jnp.concatenate(pieces, axis=1), v_ref[...], preferred_element_type=F32)

    @pl.when(j == pl.num_programs(1) - 1)
    def _():
        o_ref[...] = x_ref[...] + acc_s[...]


def _peer_dense(x, hn, eid, gate, u_all, v_all, *, layer, tn, te):
    n, d = x.shape
    sel = PK_HEADS * PK_TOPK
    row = lambda width: pl.BlockSpec((tn, width), lambda i, j: (i, 0))
    table = pl.BlockSpec((None, te, d), lambda i, j: (layer, j, 0))
    pitch = tn + 8
    return pl.pallas_call(
        functools.partial(_peer_dense_kernel, tn=tn, te=te, pitch=pitch),
        grid=(n // tn, N_EXPERTS // te),
        in_specs=[row(d), row(sel), row(sel), table, table, row(d)],
        out_specs=row(d),
        out_shape=jax.ShapeDtypeStruct((n, d), F32),
        scratch_shapes=[pltpu.VMEM((N_KEYS * pitch, LANES), F32), pltpu.VMEM((tn, d), F32)],
        compiler_params=_cparams(("parallel", "arbitrary"), vmem_mb=56),
        name="peer_dense",
    )(hn, eid, gate, u_all, v_all, x)


def _pad_tokens(a, tpad):
    return jnp.pad(a, ((0, 0), (0, tpad - a.shape[1]), (0, 0)))


def kernel(x_prompt, x_sample, cache_diff_k, cache_diff_v, cache_mem_k, cache_mem_v, state_hgrn, page_table, mem_prompt, norm_mix_g, w_in, hg_lb_logits, hg_onorm_g, da_qnorm_g, da_knorm_g, da_lambda, da_subln_g, mem_norm_g, w_mem_kv, mem_qnorm_g, mem_knorm_g, w_branch_hg, w_branch_da, w_branch_mem, w_out, norm_ffn_g, peer_wq, peer_subkeys, peer_u, peer_v):
    n_p, seq, d = x_prompt.shape
    n_s, t_new, _ = x_sample.shape
    rows_p = n_p * seq
    rows_s = n_s * t_new
    w = BRANCH_W
    depth = w_in.shape[0]

    w_in_b = w_in.astype(BF16)
    w_kv_b = w_mem_kv.astype(BF16)
    w_hg_b, w_da_b, w_cm_b, w_out_b = (a.astype(BF16) for a in (w_branch_hg, w_branch_da, w_branch_mem, w_out))
    wq_heads = peer_wq.astype(BF16).reshape(depth, d, PK_HEADS, LANES).transpose(0, 2, 1, 3)
    sk = peer_subkeys.astype(BF16)
    zk = jnp.zeros_like(sk[:, :, 0])
    sk_pad = jnp.concatenate([jnp.concatenate([sk[:, :, 0], zk], axis=-1),
                              jnp.concatenate([zk, sk[:, :, 1]], axis=-1)], axis=2)
    u_b = peer_u.astype(BF16)
    v_b = peer_v.astype(BF16)
    n_pool = cache_diff_k.shape[1]
    cache_kt = jnp.transpose(cache_diff_k, (0, 1, 3, 4, 5, 2)).reshape(depth, n_pool, w, PAGE_SIZE)
    cache_v = cache_diff_v.reshape(depth, n_pool, PAGE_SIZE * HEADS, HEAD_W)
    cmem_k = cache_mem_k.reshape(depth, n_s, MEM_LEN * HEADS, HEAD_W)
    cmem_v = cache_mem_v.reshape(depth, n_s, MEM_LEN * HEADS, HEAD_W)
    lb_logits = hg_lb_logits.astype(F32)
    rows_all = rows_p + rows_s
    with_sample = lambda full, part: lax.dynamic_update_slice(full, part, (rows_p, 0))

    mk_p, mv_p = _mem_kv(mem_prompt, mem_norm_g, w_kv_b, mem_knorm_g)

    x = jnp.concatenate([x_prompt.reshape(rows_p, d), x_sample.reshape(rows_s, d)], axis=0)
    row_mask = (np.arange(2 * HEADS * t_new)[:, None] // t_new) == (np.arange(w)[None, :] // DA_DK)
    row_mask = jnp.asarray(row_mask)
    tpad_h = 8
    tpad_c = 16

    hg_p, hg_s, dk_p, dv_p, dk_s, dv_s = [], [], [], [], [], []
    for l in range(depth):
        lam_init = 0.8 - 0.6 * math.exp(-0.3 * l)
        lam_p = da_lambda[l].astype(F32)
        proj = _norm_matmul(x, norm_mix_g[l], w_in_b[l], tn=ROW_TILE, tc=PROJ_COL_TILE)
        qn, kn, knb, vb, qm = _prep(proj, da_qnorm_g[l], da_knorm_g[l], mem_qnorm_g[l], tn=ROW_TILE)

        hgo_p, st_p = _hgrn(proj, lb_logits, hg_onorm_g[l], None, layer=l, n_seq=n_p, seq_len=seq,
                            rows_out=rows_all, row0=0, nb=1, tc=HGRN_BLOCK, c=HGRN_CHUNK, tvalid=HGRN_CHUNK,
                            out_dtype=BF16)
        src_s = _pad_tokens(proj[rows_p:, :4 * w].reshape(n_s, t_new, 4 * w), tpad_h).reshape(n_s * tpad_h, 4 * w)
        hgo_s, st_s = _hgrn(src_s, lb_logits, hg_onorm_g[l], state_hgrn, layer=l, n_seq=n_s, seq_len=tpad_h,
                            rows_out=n_s * tpad_h, row0=0, nb=8, tc=tpad_h, c=tpad_h, tvalid=t_new, out_dtype=F32)
        hgo_s = hgo_s.reshape(n_s, tpad_h, w)[:, :t_new].reshape(rows_s, w).astype(BF16)

        da_p = _dattn_prompt(qn, knb, vb, lam_p, da_subln_g[l], n_seq=n_p, seq_len=seq, rows_out=rows_all,
                             tq=ATTN_TILE, rc=ATTN_ROW_CHUNK, lam_init=lam_init)
        qs = qn[rows_p:].reshape(n_s, t_new, w)
        qrows = jnp.where(row_mask, jnp.tile(qs, (1, 2 * HEADS, 1)), jnp.zeros((), BF16))
        da_s = _dattn_sample(page_table, qrows, knb[rows_p:].reshape(n_s, t_new, w).astype(F32),
                             vb[rows_p:].reshape(n_s, t_new, w).astype(F32), cache_kt, cache_v, lam_p,
                             da_subln_g[l], layer=l, lam_init=lam_init)
        da_s = da_s.reshape(rows_s, w).astype(BF16)

        cm_p = _cross_prompt(qm, mk_p, mv_p, layer=l, n_seq=n_p, seq_len=seq, rows_out=rows_all, tq=ATTN_TILE)
        qm_s = _pad_tokens(qm[rows_p:].reshape(n_s, t_new, w), tpad_c)
        cm_s = _cross_sample(qm_s, cmem_k, cmem_v, layer=l)[:, :t_new].reshape(rows_s, w).astype(BF16)

        x = _merge(x, proj, with_sample(hgo_p, hgo_s), with_sample(da_p, da_s), with_sample(cm_p, cm_s),
                   w_hg_b[l], w_da_b[l], w_cm_b[l], w_out_b[l], tn=ROW_TILE)

        hn, eid, gate = _peer_topk(x, norm_ffn_g[l], wq_heads[l], sk_pad[l], tt=PEER_TOPK_TILE)
        x = _peer_dense(x, hn, eid, gate, u_b, v_b, layer=l, tn=PEER_TOKEN_TILE, te=PEER_EXPERT_TILE)

        hg_p.append(st_p)
        hg_s.append(st_s)
        dk_p.append(kn[:rows_p].reshape(n_p, seq, HEADS, 2, DA_DK))
        dv_p.append(proj[:rows_p, 6 * w:7 * w].reshape(n_p, seq, HEADS, HEAD_W))
        dk_s.append(kn[rows_p:].reshape(n_s, t_new, HEADS, 2, DA_DK))
        dv_s.append(proj[rows_p:, 6 * w:7 * w].reshape(n_s, t_new, HEADS, HEAD_W))

    mem_shape = (depth, n_p, MEM_LEN, HEADS, HEAD_W)
    return (x[:rows_p].reshape(n_p, seq, d), x[rows_p:].reshape(n_s, t_new, d),
            jnp.stack(hg_p), jnp.stack(hg_s), jnp.stack(dk_p), jnp.stack(dv_p),
            jnp.stack(dk_s), jnp.stack(dv_s), mk_p.reshape(mem_shape), mv_p.reshape(mem_shape))
```

```python
import functools
import math

import numpy as np
import jax
import jax.numpy as jnp
from jax import lax
from jax.experimental import pallas as pl
from jax.experimental.pallas import tpu as pltpu

F32 = jnp.float32
BF16 = jnp.bfloat16
I32 = jnp.int32

D_MODEL = 1024
DEPTH = 4
PAGE_SIZE = 128
HEADS = 4
HEAD_W = 128
BRANCH_W = HEADS * HEAD_W
DA_DK = 64
MEM_LEN = 256
N_BRANCH = 3
IN_COLS = 8 * BRANCH_W + N_BRANCH * D_MODEL
PK_HEADS = 8
N_KEYS = 128
N_EXPERTS = N_KEYS * N_KEYS
PK_TOPK = 16
EPS = 1e-6
MASK_VALUE = -1e30
NEG_INF = float("-inf")
LOG2_E = math.log2(math.e)

LANES = 128

ROW_TILE = 512
PROJ_COL_TILE = 1024
ATTN_TILE = 512
ATTN_ROW_CHUNK = 256
HGRN_BLOCK = 256
HGRN_CHUNK = 16
PEER_TOPK_TILE = 512
PEER_TOKEN_TILE = 256
PEER_EXPERT_TILE = 2048
NT_DIMS = (((1,), (1,)), ((), ()))
TN_DIMS = (((0,), (0,)), ((), ()))

_CAND_ROWS = 80
_ID_SENTINEL = float(1 << 20)


def _cparams(sem, vmem_mb=None):
    kw = dict(dimension_semantics=sem)
    if vmem_mb is not None:
        kw["vmem_limit_bytes"] = vmem_mb * 2 ** 20
    return pltpu.CompilerParams(**kw)


def _split_dot(x, m_bf16, terms, left=False):
    acc = None
    r = x
    for _ in range(terms):
        p = r.astype(BF16)
        r = r - p.astype(F32)
        if left:
            y = jnp.dot(m_bf16, p, preferred_element_type=F32)
        else:
            y = jnp.dot(p, m_bf16, preferred_element_type=F32)
        acc = y if acc is None else acc + y
    return acc


def _sigmoid(x):
    return 1.0 / (1.0 + jnp.exp(-x))


def _norm_matmul_kernel(x_ref, g_ref, w_ref, o_ref, hn_ref):
    @pl.when(pl.program_id(1) == 0)
    def _():
        x = x_ref[...]
        ms = jnp.mean(x * x, axis=-1, keepdims=True)
        hn_ref[...] = ((x * lax.rsqrt(ms + EPS)) * g_ref[...]).astype(BF16)

    o_ref[...] = jnp.dot(hn_ref[...], w_ref[...], preferred_element_type=F32)


def _norm_matmul(x, g, w, *, tn, tc):
    n, d = x.shape
    c = w.shape[1]
    return pl.pallas_call(
        _norm_matmul_kernel,
        grid=(n // tn, c // tc),
        in_specs=[
            pl.BlockSpec((tn, d), lambda i, j: (i, 0)),
            pl.BlockSpec((1, d), lambda i, j: (0, 0)),
            pl.BlockSpec((d, tc), lambda i, j: (0, j)),
        ],
        out_specs=pl.BlockSpec((tn, tc), lambda i, j: (i, j)),
        out_shape=jax.ShapeDtypeStruct((n, c), F32),
        scratch_shapes=[pltpu.VMEM((tn, d), BF16)],
        compiler_params=_cparams(("parallel", "arbitrary")),
        name="norm_matmul",
    )(x, g.reshape(1, d), w)


def _group_mean_matrix(width, group):
    r = np.arange(width)
    return jnp.asarray((r[:, None] // group == r[None, :] // group) / group, BF16)


def _prep_kernel(dq_ref, dk_ref, dv_ref, mq_ref, gq_ref, gk_ref, gm_ref, bd64_ref, bd128_ref,
                 qn_ref, kn_ref, knb_ref, vb_ref, qm_ref):
    def gnorm(x, bd, g):
        ms = _split_dot(x * x, bd, 3)
        return (x * lax.rsqrt(ms + EPS)) * g

    q = gnorm(dq_ref[...], bd64_ref[...], gq_ref[...])
    qn_ref[...] = (q * (DA_DK ** -0.5 * LOG2_E)).astype(BF16)
    k = gnorm(dk_ref[...], bd64_ref[...], gk_ref[...])
    kn_ref[...] = k
    knb_ref[...] = k.astype(BF16)
    vb_ref[...] = dv_ref[...].astype(BF16)
    qm_ref[...] = gnorm(mq_ref[...], bd128_ref[...], gm_ref[...]).astype(BF16)


def _prep(proj, gq, gk, gm, *, tn):
    n = proj.shape[0]
    w = BRANCH_W
    col = lambda c: pl.BlockSpec((tn, w), lambda i, c=c: (i, c))
    const = lambda shape: pl.BlockSpec(shape, lambda i: (0, 0))
    row_out = pl.BlockSpec((tn, w), lambda i: (i, 0))
    tile = lambda g: jnp.tile(g.astype(F32), w // g.shape[0]).reshape(1, w)
    return pl.pallas_call(
        _prep_kernel,
        grid=(n // tn,),
        in_specs=[col(4), col(5), col(6), col(7), const((1, w)), const((1, w)), const((1, w)),
                  const((w, w)), const((w, w))],
        out_specs=[row_out] * 5,
        out_shape=[jax.ShapeDtypeStruct((n, w), BF16), jax.ShapeDtypeStruct((n, w), F32),
                   jax.ShapeDtypeStruct((n, w), BF16), jax.ShapeDtypeStruct((n, w), BF16),
                   jax.ShapeDtypeStruct((n, w), BF16)],
        compiler_params=_cparams(("parallel",)),
        name="qk_norm",
    )(proj, proj, proj, proj, tile(gq), tile(gk), tile(gm),
      _group_mean_matrix(w, DA_DK), _group_mean_matrix(w, HEAD_W))


def _hgrn_kernel(*refs, layer, nb, tc, c, tvalid, has_s0):
    if has_s0:
        hq_ref, hf_ref, hi_ref, hg_ref, lbl_ref, og_ref, tri_ref, s0_ref = refs[:8]
        rest = refs[8:]
    else:
        hq_ref, hf_ref, hi_ref, hg_ref, lbl_ref, og_ref, tri_ref = refs[:7]
        s0_ref = None
        rest = refs[7:]
    o_ref, sout_ref, st_ref, q_s, k_s, b_s = rest
    t = pl.program_id(1)
    rows = nb * tc

    @pl.when(t == 0)
    def _():
        if has_s0:
            for ib in range(nb):
                for h in range(HEADS):
                    st_ref[ib, h] = s0_ref[ib, h].T
        else:
            st_ref[...] = jnp.zeros_like(st_ref)

    lg = lbl_ref[...]
    e = jnp.exp(lg - jnp.max(lg, axis=0, keepdims=True))
    pr = e / jnp.sum(e, axis=0, keepdims=True)
    lb = jnp.zeros((1, BRANCH_W), F32)
    for i in range(1, layer + 1):
        lb = lb + pr[i:i + 1]
    lb = jnp.clip(lb, 0.0, 1.0 - 1e-4)

    z = hf_ref[...]
    sg = _sigmoid(z)
    logf = jnp.log(lb + (1.0 - lb) * sg)
    kk = (1.0 - lb) * _sigmoid(-z)
    if tvalid < c:
        valid = (lax.broadcasted_iota(I32, (rows, BRANCH_W), 0) % c) < tvalid
        logf = jnp.where(valid, logf, 0.0)
        kk = jnp.where(valid, kk, 0.0)
    hq = hq_ref[...]
    q_s[...] = hq * _sigmoid(hq)
    k_s[...] = kk
    b_s[...] = _split_dot(logf, tri_ref[...], 3, left=True)

    og = og_ref[...]
    row_id = lax.broadcasted_iota(I32, (c, HEAD_W), 0)
    nsub = tc // c

    def sub(i, carry):
        r0 = pl.multiple_of(i * c, c)
        ib = i // nsub
        first = (i % nsub) == 0
        new_carry = []
        for h in range(HEADS):
            sl = slice(h * HEAD_W, (h + 1) * HEAD_W)
            bg = b_s[pl.ds(r0, c), sl]
            new_carry.append(bg[c - 1:c, :])
            b = bg - jnp.where(first, 0.0, carry[h])
            q = q_s[pl.ds(r0, c), sl]
            kx = k_s[pl.ds(r0, c), sl]
            iv = hi_ref[pl.ds(r0, c), sl]
            st = st_ref[ib, h]
            o_inter = lax.dot_general((q * jnp.exp(b)).astype(BF16), st.astype(BF16), NT_DIMS,
                                      preferred_element_type=F32)
            o_rows = []
            for tt in range(c):
                d = b[tt:tt + 1, :] - b
                w = jnp.exp(jnp.where(row_id <= tt, d, MASK_VALUE))
                p = (q[tt:tt + 1, :] * kx) * w
                a = jnp.sum(p, axis=-1, keepdims=True)
                o_rows.append(jnp.sum(a * iv, axis=0, keepdims=True))
            o = o_inter + jnp.concatenate(o_rows, axis=0)
            bl = b[c - 1:c, :]
            kd = kx * jnp.exp(bl - b)
            upd = lax.dot_general(iv.astype(BF16), kd.astype(BF16), TN_DIMS, preferred_element_type=F32)
            st_ref[ib, h] = jnp.exp(bl) * st + upd
            ms = jnp.mean(o * o, axis=-1, keepdims=True)
            hg = hg_ref[pl.ds(r0, c), sl]
            o_ref[pl.ds(r0, c), sl] = (((o * lax.rsqrt(ms + EPS)) * og) * (hg * _sigmoid(hg))).astype(o_ref.dtype)
        return tuple(new_carry)

    lax.fori_loop(0, nb * nsub, sub, tuple(jnp.zeros((1, HEAD_W), F32) for _ in range(HEADS)))

    @pl.when(t == pl.num_programs(1) - 1)
    def _():
        for ib in range(nb):
            for h in range(HEADS):
                sout_ref[ib, h] = st_ref[ib, h].T


def _hgrn(src, lb_logits, onorm_g, s0_all, *, layer, n_seq, seq_len, rows_out, row0, nb, tc, c, tvalid,
          out_dtype):
    rows = nb * tc
    nt = seq_len // tc
    rb0 = row0 // rows
    w = BRANCH_W
    col = lambda cc: pl.BlockSpec((rows, w), lambda b, t, cc=cc: (rb0 + b * nt + t, cc))
    r = np.arange(rows)
    tri = jnp.asarray((r[:, None] >= r[None, :]) & (r[:, None] // tc == r[None, :] // tc), BF16)
    in_specs = [col(0), col(1), col(2), col(3),
                pl.BlockSpec((DEPTH, w), lambda b, t: (0, 0)),
                pl.BlockSpec((1, HEAD_W), lambda b, t: (0, 0)),
                pl.BlockSpec((rows, rows), lambda b, t: (0, 0))]
    args = [src, src, src, src, lb_logits, onorm_g.reshape(1, HEAD_W), tri]
    has_s0 = s0_all is not None
    if has_s0:
        in_specs.append(pl.BlockSpec((None, nb, HEADS, HEAD_W, HEAD_W), lambda b, t: (layer, b, 0, 0, 0)))
        args.append(s0_all)
    kern = functools.partial(_hgrn_kernel, layer=layer, nb=nb, tc=tc, c=c, tvalid=tvalid, has_s0=has_s0)
    return pl.pallas_call(
        kern,
        grid=(n_seq // nb, nt),
        in_specs=in_specs,
        out_specs=[pl.BlockSpec((rows, w), lambda b, t: (b * nt + t, 0)),
                   pl.BlockSpec((nb, HEADS, HEAD_W, HEAD_W), lambda b, t: (b, 0, 0, 0))],
        out_shape=[jax.ShapeDtypeStruct((rows_out, w), out_dtype),
                   jax.ShapeDtypeStruct((n_seq, HEADS, HEAD_W, HEAD_W), F32)],
        scratch_shapes=[pltpu.VMEM((nb, HEADS, HEAD_W, HEAD_W), F32),
                        pltpu.VMEM((rows, w), F32), pltpu.VMEM((rows, w), F32), pltpu.VMEM((rows, w), F32)],
        compiler_params=_cparams(("parallel", "arbitrary")),
        name="hgrn2",
    )(*args)


def _lambda_value(lp, lam_init):
    a = jnp.sum(lp[0:1] * lp[1:2], axis=-1, keepdims=True)
    b = jnp.sum(lp[2:3] * lp[3:4], axis=-1, keepdims=True)
    return jnp.exp(a) - jnp.exp(b) + lam_init


def _dattn_prompt_kernel(iq_tab, ik_tab, lam_ref, q_ref, k_ref, v_ref, sg_ref, o_ref, q2_s, m_s, acc_s,
                         *, tq, rc, lam_init):
    step = pl.program_id(2)
    iq = iq_tab[step]
    ik = ik_tab[step]

    @pl.when(ik == 0)
    def _():
        q = q_ref[...].astype(F32)
        lane = lax.broadcasted_iota(I32, q.shape, 1)
        q2_s[0:tq, :] = jnp.where(lane < DA_DK, q, 0.0).astype(BF16)
        q2_s[tq:2 * tq, :] = jnp.where(lane >= DA_DK, q, 0.0).astype(BF16)
        m_s[...] = jnp.full_like(m_s, NEG_INF)
        acc_s[...] = jnp.zeros_like(acc_s)

    def update(masked):
        v1 = jnp.concatenate([v_ref[...], jnp.ones((tq, LANES), BF16)], axis=1)
        for c0 in range(0, 2 * tq, rc):
            rows = slice(c0, c0 + rc)
            nk = (c0 % tq) + rc if masked else tq
            s = lax.dot_general(q2_s[rows, :], k_ref[0:nk, :], NT_DIMS, preferred_element_type=F32)
            if masked:
                r = lax.broadcasted_iota(I32, s.shape, 0) + (c0 % tq)
                cidx = lax.broadcasted_iota(I32, s.shape, 1)
                s = jnp.where(cidx <= r, s, MASK_VALUE)
            m_prev = m_s[rows, :]
            m_new = jnp.maximum(m_prev, jnp.max(s, axis=-1, keepdims=True))
            alpha = jnp.exp2(m_prev - m_new)
            p = jnp.exp2(s - jnp.tile(m_new, (1, nk // LANES)))
            acc_s[rows, :] = (jnp.tile(alpha, (1, 2)) * acc_s[rows, :]
                              + jnp.dot(p.astype(BF16), v1[0:nk, :], preferred_element_type=F32))
            m_s[rows, :] = m_new

    @pl.when(ik < iq)
    def _():
        update(False)

    @pl.when(ik == iq)
    def _():
        update(True)
        lam = _lambda_value(lam_ref[...], lam_init)
        o0 = acc_s[0:tq, 0:HEAD_W] / acc_s[0:tq, HEAD_W:]
        o1 = acc_s[tq:2 * tq, 0:HEAD_W] / acc_s[tq:2 * tq, HEAD_W:]
        da = o0 - lam * o1
        ms = jnp.mean(da * da, axis=-1, keepdims=True)
        o_ref[...] = (((da * lax.rsqrt(ms + EPS)) * sg_ref[...]) * (1.0 - lam_init)).astype(o_ref.dtype)


def _dattn_prompt(qn, knb, vb, lam_p, subln_g, *, n_seq, seq_len, rows_out, tq, rc, lam_init):
    nq = seq_len // tq
    pairs = [(iq, ik) for iq in range(nq) for ik in range(iq + 1)]
    iq_tab = jnp.asarray([p[0] for p in pairs], I32)
    ik_tab = jnp.asarray([p[1] for p in pairs], I32)
    qspec = pl.BlockSpec((tq, HEAD_W), lambda b, h, s, iqt, ikt: (b * nq + iqt[s], h))
    kspec = pl.BlockSpec((tq, HEAD_W), lambda b, h, s, iqt, ikt: (b * nq + ikt[s], h))
    grid_spec = pltpu.PrefetchScalarGridSpec(
        num_scalar_prefetch=2,
        grid=(n_seq, HEADS, len(pairs)),
        in_specs=[pl.BlockSpec((4, DA_DK), lambda b, h, s, iqt, ikt: (0, 0)), qspec, kspec, kspec,
                  pl.BlockSpec((1, HEAD_W), lambda b, h, s, iqt, ikt: (0, 0))],
        out_specs=qspec,
        scratch_shapes=[pltpu.VMEM((2 * tq, HEAD_W), BF16), pltpu.VMEM((2 * tq, LANES), F32),
                        pltpu.VMEM((2 * tq, HEAD_W + LANES), F32)],
    )
    return pl.pallas_call(
        functools.partial(_dattn_prompt_kernel, tq=tq, rc=rc, lam_init=lam_init),
        grid_spec=grid_spec,
        out_shape=jax.ShapeDtypeStruct((rows_out, BRANCH_W), BF16),
        compiler_params=_cparams(("parallel", "parallel", "arbitrary")),
        name="diff_attn_prompt",
    )(iq_tab, ik_tab, lam_p, qn, knb, vb, subln_g.reshape(1, HEAD_W))


def _dattn_sample_kernel(pt_ref, lam_ref, qr_ref, kc_ref, vc_ref, sg_ref, *rest, n_pages, t_new, lam_init):
    k_refs = rest[:n_pages]
    v_refs = rest[n_pages:2 * n_pages]
    o_ref = rest[2 * n_pages]
    nrow = 2 * HEADS * t_new
    qr = qr_ref[...]
    s_past = jnp.concatenate(
        [jnp.dot(qr, k[...].astype(BF16), preferred_element_type=F32) for k in k_refs], axis=1)

    qf = qr.astype(F32)
    kc = kc_ref[...]
    vc = vc_ref[...]
    t_of_row = lax.broadcasted_iota(I32, (nrow, 1), 0) % t_new
    s_new = []
    for j in range(t_new):
        sj = jnp.sum(qf * kc[j:j + 1, :], axis=-1, keepdims=True)
        s_new.append(jnp.where(t_of_row >= j, sj, MASK_VALUE))
    m = jnp.max(s_past, axis=-1, keepdims=True)
    for sj in s_new:
        m = jnp.maximum(m, sj)
    p_past = jnp.exp2(s_past - m)
    p_new = [jnp.exp2(sj - m) for sj in s_new]
    l = jnp.sum(p_past, axis=-1, keepdims=True)
    for pj in p_new:
        l = l + pj

    lam = _lambda_value(lam_ref[...], lam_init)
    rows_h = 2 * t_new
    for h in range(HEADS):
        sl = slice(h * HEAD_W, (h + 1) * HEAD_W)
        r0 = h * rows_h
        v_h = jnp.concatenate([v[pl.ds(h, PAGE_SIZE, stride=HEADS), :] for v in v_refs], axis=0)
        acc = jnp.dot(p_past[r0:r0 + rows_h].astype(BF16), v_h.astype(BF16), preferred_element_type=F32)
        for j in range(t_new):
            acc = acc + p_new[j][r0:r0 + rows_h] * vc[j:j + 1, sl]
        on = acc / l[r0:r0 + rows_h]
        da = on[:t_new] - lam * on[t_new:]
        ms = jnp.mean(da * da, axis=-1, keepdims=True)
        o_ref[:, sl] = (((da * lax.rsqrt(ms + EPS)) * sg_ref[...]) * (1.0 - lam_init)).astype(o_ref.dtype)


def _dattn_sample(page_table, qrows, kcur, vcur, cache_kt, cache_v, lam_p, subln_g, *, layer, lam_init):
    n_seq, n_pages = page_table.shape
    t_new = kcur.shape[1]
    nrow = qrows.shape[1]
    per_seq = lambda shape: pl.BlockSpec((None,) + shape, lambda b, pt: (b, 0, 0))
    page = lambda p: pl.BlockSpec((None, None, PAGE_SIZE * HEADS, HEAD_W),
                                  lambda b, pt, p=p: (layer, pt[b * n_pages + p], 0, 0))
    grid_spec = pltpu.PrefetchScalarGridSpec(
        num_scalar_prefetch=1,
        grid=(n_seq,),
        in_specs=[pl.BlockSpec((4, DA_DK), lambda b, pt: (0, 0)),
                  per_seq((nrow, BRANCH_W)), per_seq((t_new, BRANCH_W)), per_seq((t_new, BRANCH_W)),
                  pl.BlockSpec((1, HEAD_W), lambda b, pt: (0, 0))]
                 + [page(p) for p in range(n_pages)] * 2,
        out_specs=per_seq((t_new, BRANCH_W)),
    )
    return pl.pallas_call(
        functools.partial(_dattn_sample_kernel, n_pages=n_pages, t_new=t_new, lam_init=lam_init),
        grid_spec=grid_spec,
        out_shape=jax.ShapeDtypeStruct((n_seq, t_new, BRANCH_W), F32),
        compiler_params=_cparams(("parallel",), vmem_mb=40),
        name="diff_attn_sample",
    )(page_table.reshape(-1), lam_p, qrows, kcur, vcur, subln_g.reshape(1, HEAD_W),
      *([cache_kt] * n_pages), *([cache_v] * n_pages))


def _mem_kv_kernel(m_ref, g_ref, w_ref, kg_ref, bd_ref, mk_ref, mv_ref):
    x = m_ref[...]
    ms = jnp.mean(x * x, axis=-1, keepdims=True)
    hn = ((x * lax.rsqrt(ms + EPS)) * g_ref[...]).astype(BF16)
    kv = jnp.dot(hn, w_ref[...], preferred_element_type=F32)
    k = kv[:, :BRANCH_W]
    ms = _split_dot(k * k, bd_ref[...], 3)
    mk_ref[...] = (k * lax.rsqrt(ms + EPS)) * kg_ref[...]
    mv_ref[...] = kv[:, BRANCH_W:]


def _mem_kv(mem, norm_g, w_kv, knorm_g):
    n_seq = mem.shape[0]
    w = BRANCH_W
    out = pl.BlockSpec((None, None, MEM_LEN, w), lambda l, b: (l, b, 0, 0))
    return pl.pallas_call(
        _mem_kv_kernel,
        grid=(DEPTH, n_seq),
        in_specs=[pl.BlockSpec((None, MEM_LEN, D_MODEL), lambda l, b: (b, 0, 0)),
                  pl.BlockSpec((None, 1, D_MODEL), lambda l, b: (l, 0, 0)),
                  pl.BlockSpec((None, D_MODEL, 2 * w), lambda l, b: (l, 0, 0)),
                  pl.BlockSpec((None, 1, w), lambda l, b: (l, 0, 0)),
                  pl.BlockSpec((w, w), lambda l, b: (0, 0))],
        out_specs=[out, out],
        out_shape=[jax.ShapeDtypeStruct((DEPTH, n_seq, MEM_LEN, w), F32)] * 2,
        compiler_params=_cparams(("parallel", "parallel")),
        name="mem_kv",
    )(mem, norm_g.reshape(DEPTH, 1, D_MODEL), w_kv,
      jnp.tile(knorm_g.astype(F32), (1, HEADS)).reshape(DEPTH, 1, w), _group_mean_matrix(w, HEAD_W))


def _cross_kernel(q_ref, mk_ref, mv_ref, o_ref, *, head_rows):
    q = q_ref[...]
    for h in range(HEADS):
        sl = slice(h * HEAD_W, (h + 1) * HEAD_W)
        if head_rows:
            mk = mk_ref[pl.ds(h, MEM_LEN, stride=HEADS), :].astype(BF16)
            mv = mv_ref[pl.ds(h, MEM_LEN, stride=HEADS), :].astype(BF16)
        else:
            mk = mk_ref[:, sl].astype(BF16)
            mv = mv_ref[:, sl].astype(BF16)
        s = lax.dot_general(q[:, sl], mk, NT_DIMS, preferred_element_type=F32) * (HEAD_W ** -0.5)
        p = jnp.exp(s - jnp.max(s, axis=-1, keepdims=True))
        l = jnp.sum(p, axis=-1, keepdims=True)
        o = jnp.dot(p.astype(BF16), mv, preferred_element_type=F32) / l
        o_ref[:, sl] = o.astype(o_ref.dtype)


def _cross_prompt(qm, mk_all, mv_all, *, layer, n_seq, seq_len, rows_out, tq):
    nq = seq_len // tq
    mem = pl.BlockSpec((None, None, MEM_LEN, BRANCH_W), lambda b, i: (layer, b, 0, 0))
    row = pl.BlockSpec((tq, BRANCH_W), lambda b, i: (b * nq + i, 0))
    return pl.pallas_call(
        functools.partial(_cross_kernel, head_rows=False),
        grid=(n_seq, nq),
        in_specs=[row, mem, mem],
        out_specs=row,
        out_shape=jax.ShapeDtypeStruct((rows_out, BRANCH_W), BF16),
        compiler_params=_cparams(("parallel", "parallel")),
        name="cross_attn_prompt",
    )(qm, mk_all, mv_all)


def _cross_sample(qm_pad, mk_all, mv_all, *, layer):
    n_seq, tpad, w = qm_pad.shape
    mem = pl.BlockSpec((None, None, MEM_LEN * HEADS, HEAD_W), lambda b: (layer, b, 0, 0))
    row = pl.BlockSpec((None, tpad, w), lambda b: (b, 0, 0))
    return pl.pallas_call(
        functools.partial(_cross_kernel, head_rows=True),
        grid=(n_seq,),
        in_specs=[row, mem, mem],
        out_specs=row,
        out_shape=jax.ShapeDtypeStruct((n_seq, tpad, w), F32),
        compiler_params=_cparams(("parallel",)),
        name="cross_attn_sample",
    )(qm_pad, mk_all, mv_all)


def _merge_kernel(x_ref, g0_ref, g1_ref, g2_ref, a0_ref, a1_ref, a2_ref, w0_ref, w1_ref, w2_ref, wo_ref, o_ref):
    merged = None
    for g_ref, a_ref, w_ref in ((g0_ref, a0_ref, w0_ref), (g1_ref, a1_ref, w1_ref), (g2_ref, a2_ref, w2_ref)):
        y = _sigmoid(g_ref[...]) * jnp.dot(a_ref[...], w_ref[...], preferred_element_type=F32)
        merged = y if merged is None else merged + y
    o_ref[...] = x_ref[...] + jnp.dot(merged.astype(BF16), wo_ref[...], preferred_element_type=F32)


def _merge(x, proj, hg, da, cm, w_hg, w_da, w_mem, w_out, *, tn):
    n, d = x.shape
    w = BRANCH_W
    gate_col0 = (8 * w) // d
    row = lambda width: pl.BlockSpec((tn, width), lambda i: (i, 0))
    gate = lambda c: pl.BlockSpec((tn, d), lambda i, c=c: (i, gate_col0 + c))
    const = lambda shape: pl.BlockSpec(shape, lambda i: (0, 0))
    return pl.pallas_call(
        _merge_kernel,
        grid=(n // tn,),
        in_specs=[row(d), gate(0), gate(1), gate(2), row(w), row(w), row(w),
                  const((w, d)), const((w, d)), const((w, d)), const((d, d))],
        out_specs=row(d),
        out_shape=jax.ShapeDtypeStruct((n, d), F32),
        compiler_params=_cparams(("parallel",), vmem_mb=48),
        name="branch_merge",
    )(x, proj, proj, proj, hg, da, cm, w_hg, w_da, w_mem, w_out)


def _topk_rows(s, k, key_id):
    vals, ids = [], []
    for _ in range(k):
        m = jnp.max(s, axis=0, keepdims=True)
        sel = jnp.min(jnp.where(s == m, key_id, _ID_SENTINEL), axis=0, keepdims=True)
        s = jnp.where(key_id == sel, NEG_INF, s)
        vals.append(m)
        ids.append(sel)
    return vals, ids


def _candidate_tables(tt):
    flat = np.full((_CAND_ROWS,), _ID_SENTINEL, np.float64)
    flat[0:16] = np.arange(16)
    flat[16:24] = 16 + np.arange(8)
    for j1 in range(2, 8):
        lim = PK_TOPK // (j1 + 1)
        flat[24 + (j1 - 2) * 8:24 + (j1 - 2) * 8 + lim] = j1 * 16 + np.arange(lim)
    flat[72:80] = (8 + np.arange(8)) * 16
    return jnp.asarray(np.broadcast_to(flat[:, None], (_CAND_ROWS, tt)), F32)


def _peer_topk_kernel(x_ref, g_ref, wq_ref, sk_ref, cflat_ref, hn_ref, eid_ref, gate_ref, eid_t, gate_t):
    x = x_ref[...]
    ms = jnp.mean(x * x, axis=-1, keepdims=True)
    hn = ((x * lax.rsqrt(ms + EPS)) * g_ref[...]).astype(BF16)
    hn_ref[...] = hn
    tt = x.shape[0]
    key_id = lax.broadcasted_iota(I32, (N_KEYS, tt), 0).astype(F32)
    cflat = cflat_ref[...]
    cvalid = cflat < float(PK_TOPK * PK_TOPK)

    def head(h, carry):
        qh = jnp.dot(hn, wq_ref[h], preferred_element_type=F32).astype(BF16)
        sc = lax.dot_general(sk_ref[h], qh, NT_DIMS, preferred_element_type=F32)
        v1, i1 = _topk_rows(sc[:N_KEYS], PK_TOPK, key_id)
        v2, i2 = _topk_rows(sc[N_KEYS:], PK_TOPK, key_id)
        v2a = jnp.concatenate(v2, axis=0)
        i2a = jnp.concatenate(i2, axis=0)
        pieces_v = [v1[0] + v2a, v1[1] + v2a[:8]]
        pieces_e = [i1[0] * N_KEYS + i2a, i1[1] * N_KEYS + i2a[:8]]
        for j1 in range(2, 8):
            pieces_v.append(v1[j1] + v2a[:8])
            pieces_e.append(i1[j1] * N_KEYS + i2a[:8])
        pieces_v.append(jnp.concatenate(v1[8:], axis=0) + v2a[:1])
        pieces_e.append(jnp.concatenate(i1[8:], axis=0) * N_KEYS + i2a[:1])
        cand = jnp.where(cvalid, jnp.concatenate(pieces_v, axis=0), NEG_INF)
        ce = jnp.concatenate(pieces_e, axis=0)
        fv, fe = [], []
        for _ in range(PK_TOPK):
            m = jnp.max(cand, axis=0, keepdims=True)
            sel = jnp.min(jnp.where(cand == m, cflat, _ID_SENTINEL), axis=0, keepdims=True)
            hit = cflat == sel
            fe.append(jnp.max(jnp.where(hit, ce, -1.0), axis=0, keepdims=True))
            cand = jnp.where(hit, NEG_INF, cand)
            fv.append(m)
        fva = jnp.concatenate(fv, axis=0)
        ex = jnp.exp(fva - fva[0:1])
        r0 = pl.multiple_of(h * PK_TOPK, PK_TOPK)
        gate_t[pl.ds(r0, PK_TOPK), :] = ex / jnp.sum(ex, axis=0, keepdims=True)
        eid_t[pl.ds(r0, PK_TOPK), :] = jnp.concatenate(fe, axis=0)
        return carry

    lax.fori_loop(0, PK_HEADS, head, 0)
    eid_ref[...] = eid_t[...].T.astype(I32)
    gate_ref[...] = gate_t[...].T


def _peer_topk(x, g, wq_heads, sk_pad, *, tt):
    n, d = x.shape
    sel = PK_HEADS * PK_TOPK
    return pl.pallas_call(
        _peer_topk_kernel,
        grid=(n // tt,),
        in_specs=[pl.BlockSpec((tt, d), lambda i: (i, 0)),
                  pl.BlockSpec((1, d), lambda i: (0, 0)),
                  pl.BlockSpec((PK_HEADS, d, LANES), lambda i: (0, 0, 0)),
                  pl.BlockSpec((PK_HEADS, 2 * N_KEYS, LANES), lambda i: (0, 0, 0)),
                  pl.BlockSpec((_CAND_ROWS, tt), lambda i: (0, 0))],
        out_specs=[pl.BlockSpec((tt, d), lambda i: (i, 0)),
                   pl.BlockSpec((tt, sel), lambda i: (i, 0)),
                   pl.BlockSpec((tt, sel), lambda i: (i, 0))],
        out_shape=[jax.ShapeDtypeStruct((n, d), BF16), jax.ShapeDtypeStruct((n, sel), I32),
                   jax.ShapeDtypeStruct((n, sel), F32)],
        scratch_shapes=[pltpu.VMEM((sel, tt), F32), pltpu.VMEM((sel, tt), F32)],
        compiler_params=_cparams(("parallel",)),
        name="peer_topk",
    )(x, g.reshape(1, d), wq_heads, sk_pad, _candidate_tables(tt))


def _peer_dense_kernel(hn_ref, eid_ref, gate_ref, u_ref, v_ref, x_ref, o_ref, w_s, acc_s, *, tn, te, pitch):
    j = pl.program_id(1)

    @pl.when(j == 0)
    def _():
        acc_s[...] = jnp.zeros_like(acc_s)
        n_sel = PK_HEADS * PK_TOPK
        row_id = lax.broadcasted_iota(I32, (N_KEYS, n_sel), 0).astype(F32).astype(BF16)
        one = jnp.ones((N_KEYS, n_sel), BF16)
        zero = jnp.zeros((N_KEYS, n_sel), BF16)
        bf16_rows = 16

        def all_rows(v):
            return jnp.tile(jnp.broadcast_to(v, (bf16_rows, n_sel)).astype(BF16), (N_KEYS // bf16_rows, 1))

        group = 32

        def build(n8, carry):
            n0 = pl.multiple_of(n8 * group, group)
            e8 = eid_ref[pl.ds(n0, group), :]
            g8 = gate_ref[pl.ds(n0, group), :]
            for r in range(group):
                e = e8[r:r + 1, :]
                g = g8[r:r + 1, :]
                sel_a = jnp.where(row_id == all_rows((e >> 7).astype(F32)), one, zero)
                hit_b = row_id == all_rows((e & (N_KEYS - 1)).astype(F32))
                g_hi = g.astype(BF16).astype(F32)
                g_lo = g - g_hi
                lhs = jnp.concatenate([sel_a, sel_a], axis=1)
                rhs = jnp.concatenate([jnp.where(hit_b, all_rows(g_hi), zero),
                                       jnp.where(hit_b, all_rows(g_lo), zero)], axis=1)
                w = lax.dot_general(lhs, rhs, NT_DIMS, preferred_element_type=F32)
                w_s[pl.ds(n0 + r, N_KEYS, stride=pitch), :] = w
            return carry

        lax.fori_loop(0, tn // group, build, 0)

    a = lax.dot_general(hn_ref[...], u_ref[...], NT_DIMS, preferred_element_type=F32)
    pieces = []
    for r in range(te // LANES):
        ar = a[:, r * LANES:(r + 1) * LANES]
        wr = w_s[pl.ds(pl.multiple_of((j * (te // LANES) + r) * pitch, 8), tn), :]
        act = 0.5 * ar * (1.0 + lax.erf(ar * (0.5 ** 0.5)))
        pieces.append((wr * act).astype(BF16))
    acc_s[...] += jnp.dot(jnp.concatenate(pieces, axis=1), v_ref[...], preferred_element_type=F32)

    @pl.when(j == pl.num_programs(1) - 1)
    def _():
        o_ref[...] = x_ref[...] + acc_s[...]


def _peer_dense(x, hn, eid, gate, u_all, v_all, *, layer, tn, te):
    n, d = x.shape
    sel = PK_HEADS * PK_TOPK
    row = lambda width: pl.BlockSpec((tn, width), lambda i, j: (i, 0))
    table = pl.BlockSpec((None, te, d), lambda i, j: (layer, j, 0))
    pitch = tn + 8
    return pl.pallas_call(
        functools.partial(_peer_dense_kernel, tn=tn, te=te, pitch=pitch),
        grid=(n // tn, N_EXPERTS // te),
        in_specs=[row(d), row(sel), row(sel), table, table, row(d)],
        out_specs=row(d),
        out_shape=jax.ShapeDtypeStruct((n, d), F32),
        scratch_shapes=[pltpu.VMEM((N_KEYS * pitch, LANES), F32), pltpu.VMEM((tn, d), F32)],
        compiler_params=_cparams(("parallel", "arbitrary"), vmem_mb=56),
        name="peer_dense",
    )(hn, eid, gate, u_all, v_all, x)


def _pad_tokens(a, tpad):
    return jnp.pad(a, ((0, 0), (0, tpad - a.shape[1]), (0, 0)))


def kernel(x_prompt, x_sample, cache_diff_k, cache_diff_v, cache_mem_k, cache_mem_v, state_hgrn, page_table, mem_prompt, norm_mix_g, w_in, hg_lb_logits, hg_onorm_g, da_qnorm_g, da_knorm_g, da_lambda, da_subln_g, mem_norm_g, w_mem_kv, mem_qnorm_g, mem_knorm_g, w_branch_hg, w_branch_da, w_branch_mem, w_out, norm_ffn_g, peer_wq, peer_subkeys, peer_u, peer_v):
    n_p, seq, d = x_prompt.shape
    n_s, t_new, _ = x_sample.shape
    rows_p = n_p * seq
    rows_s = n_s * t_new
    w = BRANCH_W
    depth = w_in.shape[0]

    w_in_b = w_in.astype(BF16)
    w_kv_b = w_mem_kv.astype(BF16)
    w_hg_b, w_da_b, w_cm_b, w_out_b = (a.astype(BF16) for a in (w_branch_hg, w_branch_da, w_branch_mem, w_out))
    wq_heads = peer_wq.astype(BF16).reshape(depth, d, PK_HEADS, LANES).transpose(0, 2, 1, 3)
    sk = peer_subkeys.astype(BF16)
    zk = jnp.zeros_like(sk[:, :, 0])
    sk_pad = jnp.concatenate([jnp.concatenate([sk[:, :, 0], zk], axis=-1),
                              jnp.concatenate([zk, sk[:, :, 1]], axis=-1)], axis=2)
    u_b = peer_u.astype(BF16)
    v_b = peer_v.astype(BF16)
    n_pool = cache_diff_k.shape[1]
    cache_kt = jnp.transpose(cache_diff_k, (0, 1, 3, 4, 5, 2)).reshape(depth, n_pool, w, PAGE_SIZE)
    cache_v = cache_diff_v.reshape(depth, n_pool, PAGE_SIZE * HEADS, HEAD_W)
    cmem_k = cache_mem_k.reshape(depth, n_s, MEM_LEN * HEADS, HEAD_W)
    cmem_v = cache_mem_v.reshape(depth, n_s, MEM_LEN * HEADS, HEAD_W)
    lb_logits = hg_lb_logits.astype(F32)
    rows_all = rows_p + rows_s
    with_sample = lambda full, part: lax.dynamic_update_slice(full, part, (rows_p, 0))

    mk_p, mv_p = _mem_kv(mem_prompt, mem_norm_g, w_kv_b, mem_knorm_g)

    x = jnp.concatenate([x_prompt.reshape(rows_p, d), x_sample.reshape(rows_s, d)], axis=0)
    row_mask = (np.arange(2 * HEADS * t_new)[:, None] // t_new) == (np.arange(w)[None, :] // DA_DK)
    row_mask = jnp.asarray(row_mask)
    tpad_h = 8
    tpad_c = 16

    hg_p, hg_s, dk_p, dv_p, dk_s, dv_s = [], [], [], [], [], []
    for l in range(depth):
        lam_init = 0.8 - 0.6 * math.exp(-0.3 * l)
        lam_p = da_lambda[l].astype(F32)
        proj = _norm_matmul(x, norm_mix_g[l], w_in_b[l], tn=ROW_TILE, tc=PROJ_COL_TILE)
        qn, kn, knb, vb, qm = _prep(proj, da_qnorm_g[l], da_knorm_g[l], mem_qnorm_g[l], tn=ROW_TILE)

        hgo_p, st_p = _hgrn(proj, lb_logits, hg_onorm_g[l], None, layer=l, n_seq=n_p, seq_len=seq,
                            rows_out=rows_all, row0=0, nb=1, tc=HGRN_BLOCK, c=HGRN_CHUNK, tvalid=HGRN_CHUNK,
                            out_dtype=BF16)
        src_s = _pad_tokens(proj[rows_p:, :4 * w].reshape(n_s, t_new, 4 * w), tpad_h).reshape(n_s * tpad_h, 4 * w)
        hgo_s, st_s = _hgrn(src_s, lb_logits, hg_onorm_g[l], state_hgrn, layer=l, n_seq=n_s, seq_len=tpad_h,
                            rows_out=n_s * tpad_h, row0=0, nb=8, tc=tpad_h, c=tpad_h, tvalid=t_new, out_dtype=F32)
        hgo_s = hgo_s.reshape(n_s, tpad_h, w)[:, :t_new].reshape(rows_s, w).astype(BF16)

        da_p = _dattn_prompt(qn, knb, vb, lam_p, da_subln_g[l], n_seq=n_p, seq_len=seq, rows_out=rows_all,
                             tq=ATTN_TILE, rc=ATTN_ROW_CHUNK, lam_init=lam_init)
        qs = qn[rows_p:].reshape(n_s, t_new, w)
        qrows = jnp.where(row_mask, jnp.tile(qs, (1, 2 * HEADS, 1)), jnp.zeros((), BF16))
        da_s = _dattn_sample(page_table, qrows, knb[rows_p:].reshape(n_s, t_new, w).astype(F32),
                             vb[rows_p:].reshape(n_s, t_new, w).astype(F32), cache_kt, cache_v, lam_p,
                             da_subln_g[l], layer=l, lam_init=lam_init)
        da_s = da_s.reshape(rows_s, w).astype(BF16)

        cm_p = _cross_prompt(qm, mk_p, mv_p, layer=l, n_seq=n_p, seq_len=seq, rows_out=rows_all, tq=ATTN_TILE)
        qm_s = _pad_tokens(qm[rows_p:].reshape(n_s, t_new, w), tpad_c)
        cm_s = _cross_sample(qm_s, cmem_k, cmem_v, layer=l)[:, :t_new].reshape(rows_s, w).astype(BF16)

        x = _merge(x, proj, with_sample(hgo_p, hgo_s), with_sample(da_p, da_s), with_sample(cm_p, cm_s),
                   w_hg_b[l], w_da_b[l], w_cm_b[l], w_out_b[l], tn=ROW_TILE)

        hn, eid, gate = _peer_topk(x, norm_ffn_g[l], wq_heads[l], sk_pad[l], tt=PEER_TOPK_TILE)
        x = _peer_dense(x, hn, eid, gate, u_b, v_b, layer=l, tn=PEER_TOKEN_TILE, te=PEER_EXPERT_TILE)

        hg_p.append(st_p)
        hg_s.append(st_s)
        dk_p.append(kn[:rows_p].reshape(n_p, seq, HEADS, 2, DA_DK))
        dv_p.append(proj[:rows_p, 6 * w:7 * w].reshape(n_p, seq, HEADS, HEAD_W))
        dk_s.append(kn[rows_p:].reshape(n_s, t_new, HEADS, 2, DA_DK))
        dv_s.append(proj[rows_p:, 6 * w:7 * w].reshape(n_s, t_new, HEADS, HEAD_W))

    mem_shape = (depth, n_p, MEM_LEN, HEADS, HEAD_W)
    return (x[:rows_p].reshape(n_p, seq, d), x[rows_p:].reshape(n_s, t_new, d),
            jnp.stack(hg_p), jnp.stack(hg_s), jnp.stack(dk_p), jnp.stack(dv_p),
            jnp.stack(dk_s), jnp.stack(dv_s), mk_p.reshape(mem_shape), mv_p.reshape(mem_shape))
```

```python
import functools
import math

import numpy as np
import jax
import jax.numpy as jnp
from jax import lax
from jax.experimental import pallas as pl
from jax.experimental.pallas import tpu as pltpu

F32 = jnp.float32
BF16 = jnp.bfloat16
I32 = jnp.int32

D_MODEL = 1024
DEPTH = 4
PAGE_SIZE = 128
HEADS = 4
HEAD_W = 128
BRANCH_W = HEADS * HEAD_W
DA_DK = 64
MEM_LEN = 256
N_BRANCH = 3
IN_COLS = 8 * BRANCH_W + N_BRANCH * D_MODEL
PK_HEADS = 8
N_KEYS = 128
N_EXPERTS = N_KEYS * N_KEYS
PK_TOPK = 16
EPS = 1e-6
MASK_VALUE = -1e30
NEG_INF = float("-inf")
LOG2_E = math.log2(math.e)

LANES = 128

ROW_TILE = 512
PROJ_ROW_TILE = 768
PROJ_COL_TILE = 1024
ATTN_TILE = 512
ATTN_ROW_CHUNK = 256
HGRN_BLOCK = 256
HGRN_CHUNK = 16
PEER_TOKEN_TILE = 256
PEER_EXPERT_TILE = 2048
NT_DIMS = (((1,), (1,)), ((), ()))
TN_DIMS = (((0,), (0,)), ((), ()))

_CAND_ROWS = 80
_ID_SENTINEL = float(1 << 20)


def _cparams(sem, vmem_mb=None):
    kw = dict(dimension_semantics=sem)
    if vmem_mb is not None:
        kw["vmem_limit_bytes"] = vmem_mb * 2 ** 20
    return pltpu.CompilerParams(**kw)


def _split_dot(x, m_bf16, terms, left=False):
    acc = None
    r = x
    for _ in range(terms):
        p = r.astype(BF16)
        r = r - p.astype(F32)
        if left:
            y = jnp.dot(m_bf16, p, preferred_element_type=F32)
        else:
            y = jnp.dot(p, m_bf16, preferred_element_type=F32)
        acc = y if acc is None else acc + y
    return acc


def _sigmoid(x):
    return 1.0 / (1.0 + jnp.exp(-x))


def _norm_matmul_kernel(x_ref, g_ref, w_ref, o_ref, hn_ref):
    @pl.when(pl.program_id(1) == 0)
    def _():
        x = x_ref[...]
        ms = jnp.mean(x * x, axis=-1, keepdims=True)
        hn_ref[...] = ((x * lax.rsqrt(ms + EPS)) * g_ref[...]).astype(BF16)

    o_ref[...] = jnp.dot(hn_ref[...], w_ref[...], preferred_element_type=F32)


def _norm_matmul(x, g, w, *, tn, tc):
    n, d = x.shape
    c = w.shape[1]
    return pl.pallas_call(
        _norm_matmul_kernel,
        grid=(n // tn, c // tc),
        in_specs=[
            pl.BlockSpec((tn, d), lambda i, j: (i, 0)),
            pl.BlockSpec((1, d), lambda i, j: (0, 0)),
            pl.BlockSpec((d, tc), lambda i, j: (0, j)),
        ],
        out_specs=pl.BlockSpec((tn, tc), lambda i, j: (i, j)),
        out_shape=jax.ShapeDtypeStruct((n, c), F32),
        scratch_shapes=[pltpu.VMEM((tn, d), BF16)],
        compiler_params=_cparams(("parallel", "arbitrary")),
        name="norm_matmul",
    )(x, g.reshape(1, d), w)


def _group_mean_matrix(width, group):
    r = np.arange(width)
    return jnp.asarray((r[:, None] // group == r[None, :] // group) / group, BF16)


def _prep_kernel(dq_ref, dk_ref, dv_ref, mq_ref, gq_ref, gk_ref, gm_ref, bd64_ref, bd128_ref,
                 qn_ref, kn_ref, knb_ref, vb_ref, qm_ref):
    def gnorm(x, bd, g):
        ms = _split_dot(x * x, bd, 3)
        return (x * lax.rsqrt(ms + EPS)) * g

    q = gnorm(dq_ref[...], bd64_ref[...], gq_ref[...])
    qn_ref[...] = (q * (DA_DK ** -0.5 * LOG2_E)).astype(BF16)
    k = gnorm(dk_ref[...], bd64_ref[...], gk_ref[...])
    kn_ref[...] = k
    knb_ref[...] = k.astype(BF16)
    vb_ref[...] = dv_ref[...].astype(BF16)
    qm_ref[...] = gnorm(mq_ref[...], bd128_ref[...], gm_ref[...]).astype(BF16)


def _prep(proj, gq, gk, gm, *, tn):
    n = proj.shape[0]
    w = BRANCH_W
    col = lambda c: pl.BlockSpec((tn, w), lambda i, c=c: (i, c))
    const = lambda shape: pl.BlockSpec(shape, lambda i: (0, 0))
    row_out = pl.BlockSpec((tn, w), lambda i: (i, 0))
    tile = lambda g: jnp.tile(g.astype(F32), w // g.shape[0]).reshape(1, w)
    return pl.pallas_call(
        _prep_kernel,
        grid=(n // tn,),
        in_specs=[col(4), col(5), col(6), col(7), const((1, w)), const((1, w)), const((1, w)),
                  const((w, w)), const((w, w))],
        out_specs=[row_out] * 5,
        out_shape=[jax.ShapeDtypeStruct((n, w), BF16), jax.ShapeDtypeStruct((n, w), F32),
                   jax.ShapeDtypeStruct((n, w), BF16), jax.ShapeDtypeStruct((n, w), BF16),
                   jax.ShapeDtypeStruct((n, w), BF16)],
        compiler_params=_cparams(("parallel",)),
        name="qk_norm",
    )(proj, proj, proj, proj, tile(gq), tile(gk), tile(gm),
      _group_mean_matrix(w, DA_DK), _group_mean_matrix(w, HEAD_W))


def _hgrn_kernel(*refs, layer, nb, tc, c, tvalid, has_s0):
    if has_s0:
        hq_ref, hf_ref, hi_ref, hg_ref, lbl_ref, og_ref, tri_ref, s0_ref = refs[:8]
        rest = refs[8:]
    else:
        hq_ref, hf_ref, hi_ref, hg_ref, lbl_ref, og_ref, tri_ref = refs[:7]
        s0_ref = None
        rest = refs[7:]
    o_ref, sout_ref, st_ref, q_s, k_s, b_s = rest
    t = pl.program_id(1)
    rows = nb * tc

    @pl.when(t == 0)
    def _():
        if has_s0:
            for ib in range(nb):
                for h in range(HEADS):
                    st_ref[ib, h] = s0_ref[ib, h].T
        else:
            st_ref[...] = jnp.zeros_like(st_ref)

    lg = lbl_ref[...]
    e = jnp.exp(lg - jnp.max(lg, axis=0, keepdims=True))
    pr = e / jnp.sum(e, axis=0, keepdims=True)
    lb = jnp.zeros((1, BRANCH_W), F32)
    for i in range(1, layer + 1):
        lb = lb + pr[i:i + 1]
    lb = jnp.clip(lb, 0.0, 1.0 - 1e-4)

    z = hf_ref[...]
    sg = _sigmoid(z)
    logf = jnp.log(lb + (1.0 - lb) * sg)
    kk = (1.0 - lb) * _sigmoid(-z)
    if tvalid < c:
        valid = (lax.broadcasted_iota(I32, (rows, BRANCH_W), 0) % c) < tvalid
        logf = jnp.where(valid, logf, 0.0)
        kk = jnp.where(valid, kk, 0.0)
    hq = hq_ref[...]
    q_s[...] = hq * _sigmoid(hq)
    k_s[...] = kk
    b_s[...] = _split_dot(logf, tri_ref[...], 3, left=True)

    og = og_ref[...]
    row_id = lax.broadcasted_iota(I32, (c, HEAD_W), 0)
    nsub = tc // c

    def sub(i, carry):
        r0 = pl.multiple_of(i * c, c)
        ib = i // nsub
        first = (i % nsub) == 0
        new_carry = []
        for h in range(HEADS):
            sl = slice(h * HEAD_W, (h + 1) * HEAD_W)
            bg = b_s[pl.ds(r0, c), sl]
            new_carry.append(bg[c - 1:c, :])
            b = bg - jnp.where(first, 0.0, carry[h])
            q = q_s[pl.ds(r0, c), sl]
            kx = k_s[pl.ds(r0, c), sl]
            iv = hi_ref[pl.ds(r0, c), sl]
            st = st_ref[ib, h]
            o_inter = lax.dot_general((q * jnp.exp(b)).astype(BF16), st.astype(BF16), NT_DIMS,
                                      preferred_element_type=F32)
            o_rows = []
            for tt in range(c):
                d = b[tt:tt + 1, :] - b
                w = jnp.exp(jnp.where(row_id <= tt, d, MASK_VALUE))
                p = (q[tt:tt + 1, :] * kx) * w
                a = jnp.sum(p, axis=-1, keepdims=True)
                o_rows.append(jnp.sum(a * iv, axis=0, keepdims=True))
            o = o_inter + jnp.concatenate(o_rows, axis=0)
            bl = b[c - 1:c, :]
            kd = kx * jnp.exp(bl - b)
            upd = lax.dot_general(iv.astype(BF16), kd.astype(BF16), TN_DIMS, preferred_element_type=F32)
            st_ref[ib, h] = jnp.exp(bl) * st + upd
            ms = jnp.mean(o * o, axis=-1, keepdims=True)
            hg = hg_ref[pl.ds(r0, c), sl]
            o_ref[pl.ds(r0, c), sl] = (((o * lax.rsqrt(ms + EPS)) * og) * (hg * _sigmoid(hg))).astype(o_ref.dtype)
        return tuple(new_carry)

    lax.fori_loop(0, nb * nsub, sub, tuple(jnp.zeros((1, HEAD_W), F32) for _ in range(HEADS)))

    @pl.when(t == pl.num_programs(1) - 1)
    def _():
        for ib in range(nb):
            for h in range(HEADS):
                sout_ref[ib, h] = st_ref[ib, h].T


def _hgrn(src, lb_logits, onorm_g, s0_all, *, layer, n_seq, seq_len, rows_out, row0, nb, tc, c, tvalid,
          out_dtype):
    rows = nb * tc
    nt = seq_len // tc
    rb0 = row0 // rows
    w = BRANCH_W
    col = lambda cc: pl.BlockSpec((rows, w), lambda b, t, cc=cc: (rb0 + b * nt + t, cc))
    r = np.arange(rows)
    tri = jnp.asarray((r[:, None] >= r[None, :]) & (r[:, None] // tc == r[None, :] // tc), BF16)
    in_specs = [col(0), col(1), col(2), col(3),
                pl.BlockSpec((DEPTH, w), lambda b, t: (0, 0)),
                pl.BlockSpec((1, HEAD_W), lambda b, t: (0, 0)),
                pl.BlockSpec((rows, rows), lambda b, t: (0, 0))]
    args = [src, src, src, src, lb_logits, onorm_g.reshape(1, HEAD_W), tri]
    has_s0 = s0_all is not None
    if has_s0:
        in_specs.append(pl.BlockSpec((None, nb, HEADS, HEAD_W, HEAD_W), lambda b, t: (layer, b, 0, 0, 0)))
        args.append(s0_all)
    kern = functools.partial(_hgrn_kernel, layer=layer, nb=nb, tc=tc, c=c, tvalid=tvalid, has_s0=has_s0)
    return pl.pallas_call(
        kern,
        grid=(n_seq // nb, nt),
        in_specs=in_specs,
        out_specs=[pl.BlockSpec((rows, w), lambda b, t: (b * nt + t, 0)),
                   pl.BlockSpec((nb, HEADS, HEAD_W, HEAD_W), lambda b, t: (b, 0, 0, 0))],
        out_shape=[jax.ShapeDtypeStruct((rows_out, w), out_dtype),
                   jax.ShapeDtypeStruct((n_seq, HEADS, HEAD_W, HEAD_W), F32)],
        scratch_shapes=[pltpu.VMEM((nb, HEADS, HEAD_W, HEAD_W), F32),
                        pltpu.VMEM((rows, w), F32), pltpu.VMEM((rows, w), F32), pltpu.VMEM((rows, w), F32)],
        compiler_params=_cparams(("parallel", "arbitrary")),
        name="hgrn2",
    )(*args)


def _lambda_value(lp, lam_init):
    a = jnp.sum(lp[0:1] * lp[1:2], axis=-1, keepdims=True)
    b = jnp.sum(lp[2:3] * lp[3:4], axis=-1, keepdims=True)
    return jnp.exp(a) - jnp.exp(b) + lam_init


def _dattn_prompt_kernel(iq_tab, ik_tab, lam_ref, q_ref, k_ref, v_ref, sg_ref, o_ref, q2_s, m_s, acc_s,
                         *, tq, rc, lam_init):
    step = pl.program_id(2)
    iq = iq_tab[step]
    ik = ik_tab[step]

    @pl.when(ik == 0)
    def _():
        q = q_ref[...].astype(F32)
        lane = lax.broadcasted_iota(I32, q.shape, 1)
        q2_s[0:tq, :] = jnp.where(lane < DA_DK, q, 0.0).astype(BF16)
        q2_s[tq:2 * tq, :] = jnp.where(lane >= DA_DK, q, 0.0).astype(BF16)
        m_s[...] = jnp.full_like(m_s, NEG_INF)
        acc_s[...] = jnp.zeros_like(acc_s)

    def update(masked):
        v1 = jnp.concatenate([v_ref[...], jnp.ones((tq, LANES), BF16)], axis=1)
        for c0 in range(0, 2 * tq, rc):
            rows = slice(c0, c0 + rc)
            nk = (c0 % tq) + rc if masked else tq
            s = lax.dot_general(q2_s[rows, :], k_ref[0:nk, :], NT_DIMS, preferred_element_type=F32)
            if masked:
                r = lax.broadcasted_iota(I32, s.shape, 0) + (c0 % tq)
                cidx = lax.broadcasted_iota(I32, s.shape, 1)
                s = jnp.where(cidx <= r, s, MASK_VALUE)
            m_prev = m_s[rows, :]
            m_new = jnp.maximum(m_prev, jnp.max(s, axis=-1, keepdims=True))
            alpha = jnp.exp2(m_prev - m_new)
            p = jnp.exp2(s - jnp.tile(m_new, (1, nk // LANES)))
            acc_s[rows, :] = (jnp.tile(alpha, (1, 2)) * acc_s[rows, :]
                              + jnp.dot(p.astype(BF16), v1[0:nk, :], preferred_element_type=F32))
            m_s[rows, :] = m_new

    @pl.when(ik < iq)
    def _():
        update(False)

    @pl.when(ik == iq)
    def _():
        update(True)
        lam = _lambda_value(lam_ref[...], lam_init)
        o0 = acc_s[0:tq, 0:HEAD_W] / acc_s[0:tq, HEAD_W:]
        o1 = acc_s[tq:2 * tq, 0:HEAD_W] / acc_s[tq:2 * tq, HEAD_W:]
        da = o0 - lam * o1
        ms = jnp.mean(da * da, axis=-1, keepdims=True)
        o_ref[...] = (((da * lax.rsqrt(ms + EPS)) * sg_ref[...]) * (1.0 - lam_init)).astype(o_ref.dtype)


def _dattn_prompt(qn, knb, vb, lam_p, subln_g, *, n_seq, seq_len, rows_out, tq, rc, lam_init):
    nq = seq_len // tq
    pairs = [(iq, ik) for iq in range(nq) for ik in range(iq + 1)]
    iq_tab = jnp.asarray([p[0] for p in pairs], I32)
    ik_tab = jnp.asarray([p[1] for p in pairs], I32)
    qspec = pl.BlockSpec((tq, HEAD_W), lambda b, h, s, iqt, ikt: (b * nq + iqt[s], h))
    kspec = pl.BlockSpec((tq, HEAD_W), lambda b, h, s, iqt, ikt: (b * nq + ikt[s], h))
    grid_spec = pltpu.PrefetchScalarGridSpec(
        num_scalar_prefetch=2,
        grid=(n_seq, HEADS, len(pairs)),
        in_specs=[pl.BlockSpec((4, DA_DK), lambda b, h, s, iqt, ikt: (0, 0)), qspec, kspec, kspec,
                  pl.BlockSpec((1, HEAD_W), lambda b, h, s, iqt, ikt: (0, 0))],
        out_specs=qspec,
        scratch_shapes=[pltpu.VMEM((2 * tq, HEAD_W), BF16), pltpu.VMEM((2 * tq, LANES), F32),
                        pltpu.VMEM((2 * tq, HEAD_W + LANES), F32)],
    )
    return pl.pallas_call(
        functools.partial(_dattn_prompt_kernel, tq=tq, rc=rc, lam_init=lam_init),
        grid_spec=grid_spec,
        out_shape=jax.ShapeDtypeStruct((rows_out, BRANCH_W), BF16),
        compiler_params=_cparams(("parallel", "parallel", "arbitrary")),
        name="diff_attn_prompt",
    )(iq_tab, ik_tab, lam_p, qn, knb, vb, subln_g.reshape(1, HEAD_W))


def _dattn_sample_kernel(pt_ref, lam_ref, qr_ref, kc_ref, vc_ref, sg_ref, *rest, n_pages, t_new, lam_init):
    k_refs = rest[:n_pages]
    v_refs = rest[n_pages:2 * n_pages]
    o_ref = rest[2 * n_pages]
    nrow = 2 * HEADS * t_new
    qr = qr_ref[...]
    s_past = jnp.concatenate(
        [jnp.dot(qr, k[...].astype(BF16), preferred_element_type=F32) for k in k_refs], axis=1)

    qf = qr.astype(F32)
    kc = kc_ref[...]
    vc = vc_ref[...]
    t_of_row = lax.broadcasted_iota(I32, (nrow, 1), 0) % t_new
    s_new = []
    for j in range(t_new):
        sj = jnp.sum(qf * kc[j:j + 1, :], axis=-1, keepdims=True)
        s_new.append(jnp.where(t_of_row >= j, sj, MASK_VALUE))
    m = jnp.max(s_past, axis=-1, keepdims=True)
    for sj in s_new:
        m = jnp.maximum(m, sj)
    p_past = jnp.exp2(s_past - m)
    p_new = [jnp.exp2(sj - m) for sj in s_new]
    l = jnp.sum(p_past, axis=-1, keepdims=True)
    for pj in p_new:
        l = l + pj

    lam = _lambda_value(lam_ref[...], lam_init)
    rows_h = 2 * t_new
    for h in range(HEADS):
        sl = slice(h * HEAD_W, (h + 1) * HEAD_W)
        r0 = h * rows_h
        v_h = jnp.concatenate([v[pl.ds(h, PAGE_SIZE, stride=HEADS), :] for v in v_refs], axis=0)
        acc = jnp.dot(p_past[r0:r0 + rows_h].astype(BF16), v_h.astype(BF16), preferred_element_type=F32)
        for j in range(t_new):
            acc = acc + p_new[j][r0:r0 + rows_h] * vc[j:j + 1, sl]
        on = acc / l[r0:r0 + rows_h]
        da = on[:t_new] - lam * on[t_new:]
        ms = jnp.mean(da * da, axis=-1, keepdims=True)
        o_ref[:, sl] = (((da * lax.rsqrt(ms + EPS)) * sg_ref[...]) * (1.0 - lam_init)).astype(o_ref.dtype)


def _dattn_sample(page_table, qrows, kcur, vcur, cache_kt, cache_v, lam_p, subln_g, *, layer, lam_init):
    n_seq, n_pages = page_table.shape
    t_new = kcur.shape[1]
    nrow = qrows.shape[1]
    per_seq = lambda shape: pl.BlockSpec((None,) + shape, lambda b, pt: (b, 0, 0))
    page = lambda p: pl.BlockSpec((None, None, PAGE_SIZE * HEADS, HEAD_W),
                                  lambda b, pt, p=p: (layer, pt[b * n_pages + p], 0, 0))
    grid_spec = pltpu.PrefetchScalarGridSpec(
        num_scalar_prefetch=1,
        grid=(n_seq,),
        in_specs=[pl.BlockSpec((4, DA_DK), lambda b, pt: (0, 0)),
                  per_seq((nrow, BRANCH_W)), per_seq((t_new, BRANCH_W)), per_seq((t_new, BRANCH_W)),
                  pl.BlockSpec((1, HEAD_W), lambda b, pt: (0, 0))]
                 + [page(p) for p in range(n_pages)] * 2,
        out_specs=per_seq((t_new, BRANCH_W)),
    )
    return pl.pallas_call(
        functools.partial(_dattn_sample_kernel, n_pages=n_pages, t_new=t_new, lam_init=lam_init),
        grid_spec=grid_spec,
        out_shape=jax.ShapeDtypeStruct((n_seq, t_new, BRANCH_W), F32),
        compiler_params=_cparams(("parallel",), vmem_mb=40),
        name="diff_attn_sample",
    )(page_table.reshape(-1), lam_p, qrows, kcur, vcur, subln_g.reshape(1, HEAD_W),
      *([cache_kt] * n_pages), *([cache_v] * n_pages))


def _mem_kv_kernel(m_ref, g_ref, w_ref, kg_ref, bd_ref, mk_ref, mv_ref):
    x = m_ref[...]
    ms = jnp.mean(x * x, axis=-1, keepdims=True)
    hn = ((x * lax.rsqrt(ms + EPS)) * g_ref[...]).astype(BF16)
    kv = jnp.dot(hn, w_ref[...], preferred_element_type=F32)
    k = kv[:, :BRANCH_W]
    ms = _split_dot(k * k, bd_ref[...], 3)
    mk_ref[...] = (k * lax.rsqrt(ms + EPS)) * kg_ref[...]
    mv_ref[...] = kv[:, BRANCH_W:]


def _mem_kv(mem, norm_g, w_kv, knorm_g):
    n_seq = mem.shape[0]
    w = BRANCH_W
    out = pl.BlockSpec((None, None, MEM_LEN, w), lambda l, b: (l, b, 0, 0))
    return pl.pallas_call(
        _mem_kv_kernel,
        grid=(DEPTH, n_seq),
        in_specs=[pl.BlockSpec((None, MEM_LEN, D_MODEL), lambda l, b: (b, 0, 0)),
                  pl.BlockSpec((None, 1, D_MODEL), lambda l, b: (l, 0, 0)),
                  pl.BlockSpec((None, D_MODEL, 2 * w), lambda l, b: (l, 0, 0)),
                  pl.BlockSpec((None, 1, w), lambda l, b: (l, 0, 0)),
                  pl.BlockSpec((w, w), lambda l, b: (0, 0))],
        out_specs=[out, out],
        out_shape=[jax.ShapeDtypeStruct((DEPTH, n_seq, MEM_LEN, w), F32)] * 2,
        compiler_params=_cparams(("parallel", "parallel")),
        name="mem_kv",
    )(mem, norm_g.reshape(DEPTH, 1, D_MODEL), w_kv,
      jnp.tile(knorm_g.astype(F32), (1, HEADS)).reshape(DEPTH, 1, w), _group_mean_matrix(w, HEAD_W))


def _cross_kernel(q_ref, mk_ref, mv_ref, o_ref, *, head_rows):
    q = q_ref[...]
    for h in range(HEADS):
        sl = slice(h * HEAD_W, (h + 1) * HEAD_W)
        if head_rows:
            mk = mk_ref[pl.ds(h, MEM_LEN, stride=HEADS), :].astype(BF16)
            mv = mv_ref[pl.ds(h, MEM_LEN, stride=HEADS), :].astype(BF16)
        else:
            mk = mk_ref[:, sl].astype(BF16)
            mv = mv_ref[:, sl].astype(BF16)
        s = lax.dot_general(q[:, sl], mk, NT_DIMS, preferred_element_type=F32) * (HEAD_W ** -0.5)
        p = jnp.exp(s - jnp.max(s, axis=-1, keepdims=True))
        l = jnp.sum(p, axis=-1, keepdims=True)
        o = jnp.dot(p.astype(BF16), mv, preferred_element_type=F32) / l
        o_ref[:, sl] = o.astype(o_ref.dtype)


def _cross_prompt(qm, mk_all, mv_all, *, layer, n_seq, seq_len, rows_out, tq):
    nq = seq_len // tq
    mem = pl.BlockSpec((None, None, MEM_LEN, BRANCH_W), lambda b, i: (layer, b, 0, 0))
    row = pl.BlockSpec((tq, BRANCH_W), lambda b, i: (b * nq + i, 0))
    return pl.pallas_call(
        functools.partial(_cross_kernel, head_rows=False),
        grid=(n_seq, nq),
        in_specs=[row, mem, mem],
        out_specs=row,
        out_shape=jax.ShapeDtypeStruct((rows_out, BRANCH_W), BF16),
        compiler_params=_cparams(("parallel", "parallel")),
        name="cross_attn_prompt",
    )(qm, mk_all, mv_all)


def _cross_sample(qm_pad, mk_all, mv_all, *, layer):
    n_seq, tpad, w = qm_pad.shape
    mem = pl.BlockSpec((None, None, MEM_LEN * HEADS, HEAD_W), lambda b: (layer, b, 0, 0))
    row = pl.BlockSpec((None, tpad, w), lambda b: (b, 0, 0))
    return pl.pallas_call(
        functools.partial(_cross_kernel, head_rows=True),
        grid=(n_seq,),
        in_specs=[row, mem, mem],
        out_specs=row,
        out_shape=jax.ShapeDtypeStruct((n_seq, tpad, w), F32),
        compiler_params=_cparams(("parallel",)),
        name="cross_attn_sample",
    )(qm_pad, mk_all, mv_all)


def _merge_kernel(x_ref, g0_ref, g1_ref, g2_ref, a0_ref, a1_ref, a2_ref, w0_ref, w1_ref, w2_ref, wo_ref, o_ref):
    merged = None
    for g_ref, a_ref, w_ref in ((g0_ref, a0_ref, w0_ref), (g1_ref, a1_ref, w1_ref), (g2_ref, a2_ref, w2_ref)):
        y = _sigmoid(g_ref[...]) * jnp.dot(a_ref[...], w_ref[...], preferred_element_type=F32)
        merged = y if merged is None else merged + y
    o_ref[...] = x_ref[...] + jnp.dot(merged.astype(BF16), wo_ref[...], preferred_element_type=F32)


def _merge(x, proj, hg, da, cm, w_hg, w_da, w_mem, w_out, *, tn):
    n, d = x.shape
    w = BRANCH_W
    gate_col0 = (8 * w) // d
    row = lambda width: pl.BlockSpec((tn, width), lambda i: (i, 0))
    gate = lambda c: pl.BlockSpec((tn, d), lambda i, c=c: (i, gate_col0 + c))
    const = lambda shape: pl.BlockSpec(shape, lambda i: (0, 0))
    return pl.pallas_call(
        _merge_kernel,
        grid=(n // tn,),
        in_specs=[row(d), gate(0), gate(1), gate(2), row(w), row(w), row(w),
                  const((w, d)), const((w, d)), const((w, d)), const((d, d))],
        out_specs=row(d),
        out_shape=jax.ShapeDtypeStruct((n, d), F32),
        compiler_params=_cparams(("parallel",), vmem_mb=48),
        name="branch_merge",
    )(x, proj, proj, proj, hg, da, cm, w_hg, w_da, w_mem, w_out)


def _topk_rows(s, k, key_id):
    vals, ids = [], []
    for _ in range(k):
        m = jnp.max(s, axis=0, keepdims=True)
        sel = jnp.min(jnp.where(s == m, key_id, _ID_SENTINEL), axis=0, keepdims=True)
        s = jnp.where(key_id == sel, NEG_INF, s)
        vals.append(m)
        ids.append(sel)
        yield None
    return vals, ids


def _candidate_tables(tt):
    flat = np.full((_CAND_ROWS,), _ID_SENTINEL, np.float64)
    flat[0:16] = np.arange(16)
    flat[16:24] = 16 + np.arange(8)
    for j1 in range(2, 8):
        lim = PK_TOPK // (j1 + 1)
        flat[24 + (j1 - 2) * 8:24 + (j1 - 2) * 8 + lim] = j1 * 16 + np.arange(lim)
    flat[72:80] = (8 + np.arange(8)) * 16
    return jnp.asarray(np.broadcast_to(flat[:, None], (_CAND_ROWS, tt)), F32)


def _norm_bf16(x, g):
    ms = jnp.mean(x * x, axis=-1, keepdims=True)
    return ((x * lax.rsqrt(ms + EPS)) * g).astype(BF16)


def _select_head_steps(h, hn, wq_ref, sk_ref, cflat, eid_t, gate_t, lane0=0):
    tt = hn.shape[0]
    cflat = cflat[:, :tt]
    key_id = lax.broadcasted_iota(I32, (N_KEYS, tt), 0).astype(F32)
    cvalid = cflat < float(PK_TOPK * PK_TOPK)
    qh = jnp.dot(hn, wq_ref[h], preferred_element_type=F32).astype(BF16)
    sc = lax.dot_general(sk_ref[h], qh, NT_DIMS, preferred_element_type=F32)
    yield None
    v1, i1 = yield from _topk_rows(sc[:N_KEYS], PK_TOPK, key_id)
    v2, i2 = yield from _topk_rows(sc[N_KEYS:], PK_TOPK, key_id)
    v2a = jnp.concatenate(v2, axis=0)
    i2a = jnp.concatenate(i2, axis=0)
    pieces_v = [v1[0] + v2a, v1[1] + v2a[:8]]
    pieces_e = [i1[0] * N_KEYS + i2a, i1[1] * N_KEYS + i2a[:8]]
    for j1 in range(2, 8):
        pieces_v.append(v1[j1] + v2a[:8])
        pieces_e.append(i1[j1] * N_KEYS + i2a[:8])
    pieces_v.append(jnp.concatenate(v1[8:], axis=0) + v2a[:1])
    pieces_e.append(jnp.concatenate(i1[8:], axis=0) * N_KEYS + i2a[:1])
    cand = jnp.where(cvalid, jnp.concatenate(pieces_v, axis=0), NEG_INF)
    ce = jnp.concatenate(pieces_e, axis=0)
    fv, fe = [], []
    for _ in range(PK_TOPK):
        m = jnp.max(cand, axis=0, keepdims=True)
        sel = jnp.min(jnp.where(cand == m, cflat, _ID_SENTINEL), axis=0, keepdims=True)
        hit = cflat == sel
        fe.append(jnp.max(jnp.where(hit, ce, -1.0), axis=0, keepdims=True))
        cand = jnp.where(hit, NEG_INF, cand)
        fv.append(m)
        yield None
    fva = jnp.concatenate(fv, axis=0)
    ex = jnp.exp(fva - fva[0:1])
    r0 = pl.multiple_of(h * PK_TOPK, PK_TOPK)
    gate_t[pl.ds(r0, PK_TOPK), lane0:lane0 + tt] = ex / jnp.sum(ex, axis=0, keepdims=True)
    eid_t[pl.ds(r0, PK_TOPK), lane0:lane0 + tt] = jnp.concatenate(fe, axis=0)


def _select_head(*args):
    for _ in _select_head_steps(*args):
        pass


def _build_routing(eid_s, gate_s, w_s, *, tn, pitch):
    n_sel = PK_HEADS * PK_TOPK
    row_id = lax.broadcasted_iota(I32, (N_KEYS, n_sel), 0).astype(F32).astype(BF16)
    one = jnp.ones((N_KEYS, n_sel), BF16)
    zero = jnp.zeros((N_KEYS, n_sel), BF16)
    bf16_rows = 16

    def all_rows(v):
        return jnp.tile(jnp.broadcast_to(v, (bf16_rows, n_sel)).astype(BF16), (N_KEYS // bf16_rows, 1))

    group = 32

    def build(n8, carry):
        n0 = pl.multiple_of(n8 * group, group)
        e8 = eid_s[pl.ds(n0, group), :]
        g8 = gate_s[pl.ds(n0, group), :]
        for r in range(group):
            e = e8[r:r + 1, :]
            g = 0.5 * g8[r:r + 1, :]
            a_id = jnp.floor(e * (1.0 / N_KEYS))
            sel_a = jnp.where(row_id == all_rows(a_id), one, zero)
            hit_b = row_id == all_rows(e - a_id * N_KEYS)
            g_hi = g.astype(BF16).astype(F32)
            g_lo = g - g_hi
            lhs = jnp.concatenate([sel_a, sel_a], axis=1)
            rhs = jnp.concatenate([jnp.where(hit_b, all_rows(g_hi), zero),
                                   jnp.where(hit_b, all_rows(g_lo), zero)], axis=1)
            w = lax.dot_general(lhs, rhs, NT_DIMS, preferred_element_type=F32)
            w_s[pl.ds(n0 + r, N_KEYS, stride=pitch), :] = w
        return carry

    lax.fori_loop(0, tn // group, build, 0)


def _peer_kernel(xc_ref, xn_ref, g_ref, wq_ref, sk_ref, cflat_ref, u_ref, v_ref, o_ref,
                 w_s, acc_s, hn_s, eid_t, gate_t, eid_s, gate_s, *, tn, te, pitch):
    i = pl.program_id(0)
    j = pl.program_id(1)
    cur = i % 2
    nxt = 1 - cur
    cflat = cflat_ref[...]

    @pl.when(j == 0)
    def _():
        @pl.when(i == 0)
        def _():
            hn0 = _norm_bf16(xc_ref[...], g_ref[...])
            hn_s[0] = hn0

            def head(h, carry):
                _select_head(h, hn0, wq_ref, sk_ref, cflat, eid_t, gate_t)
                return carry

            lax.fori_loop(0, PK_HEADS, head, 0)
            eid_s[...] = eid_t[...].T
            gate_s[...] = gate_t[...].T

        _build_routing(eid_s, gate_s, w_s, tn=tn, pitch=pitch)
        acc_s[...] = jnp.zeros_like(acc_s)
        hn_s[nxt] = _norm_bf16(xn_ref[...], g_ref[...])

    sub = LANES
    hn_next = hn_s[nxt]
    selects = [_select_head_steps(j, hn_next[t0:t0 + sub], wq_ref, sk_ref, cflat, eid_t, gate_t, lane0=t0)
               for t0 in range(0, tn, sub)]

    def advance(units):
        for _ in range(units):
            while selects and next(selects[0], "done") == "done":
                selects.pop(0)
                if not selects:
                    return

    advance(1)
    units_per_piece = 11
    hn = hn_s[cur]
    acc = acc_s[...]
    e_chunk = 2 * LANES
    v_chunk = 8 * e_chunk
    for c0 in range(0, te, v_chunk):
        pieces = []
        for c1 in range(c0, c0 + v_chunk, e_chunk):
            a = lax.dot_general(hn, u_ref[c1:c1 + e_chunk, :], NT_DIMS, preferred_element_type=F32)
            for r in range(e_chunk // LANES):
                ar = a[:, r * LANES:(r + 1) * LANES]
                a_id = j * (te // LANES) + c1 // LANES + r
                wr = w_s[pl.ds(pl.multiple_of(a_id * pitch, 8), tn), :]
                act = ar * (1.0 + lax.erf(ar * (0.5 ** 0.5)))
                pieces.append((wr * act).astype(BF16))
            advance(units_per_piece)
        acc = acc + jnp.dot(jnp.concatenate(pieces, axis=1), v_ref[c0:c0 + v_chunk, :],
                            preferred_element_type=F32)
        advance(units_per_piece)
    acc_s[...] = acc
    advance(len(selects) * 64)

    @pl.when(j == pl.num_programs(1) - 1)
    def _():
        o_ref[...] = xc_ref[...] + acc_s[...]
        eid_s[...] = eid_t[...].T
        gate_s[...] = gate_t[...].T


def _peer(x, g, wq_heads, sk_pad, u_all, v_all, *, layer, tn, te):
    n, d = x.shape
    nt = n // tn
    assert N_EXPERTS // te == PK_HEADS
    sel = PK_HEADS * PK_TOPK
    pitch = tn + 8
    table = pl.BlockSpec((None, te, d), lambda i, j: (layer, j, 0))
    const = lambda shape: pl.BlockSpec(shape, lambda i, j: (0,) * len(shape))
    return pl.pallas_call(
        functools.partial(_peer_kernel, tn=tn, te=te, pitch=pitch),
        grid=(nt, N_EXPERTS // te),
        in_specs=[pl.BlockSpec((tn, d), lambda i, j: (i, 0)),
                  pl.BlockSpec((tn, d), lambda i, j: (jnp.minimum(i + 1, nt - 1), 0)),
                  const((1, d)), const((PK_HEADS, d, LANES)), const((PK_HEADS, 2 * N_KEYS, LANES)),
                  const((_CAND_ROWS, tn)), table, table],
        out_specs=pl.BlockSpec((tn, d), lambda i, j: (i, 0)),
        out_shape=jax.ShapeDtypeStruct((n, d), F32),
        scratch_shapes=[pltpu.VMEM((N_KEYS * pitch, LANES), F32), pltpu.VMEM((tn, d), F32),
                        pltpu.VMEM((2, tn, d), BF16),
                        pltpu.VMEM((sel, tn), F32), pltpu.VMEM((sel, tn), F32),
                        pltpu.VMEM((tn, sel), F32), pltpu.VMEM((tn, sel), F32)],
        compiler_params=_cparams(("arbitrary", "arbitrary"), vmem_mb=60),
        name="peer",
    )(x, x, g.reshape(1, d), wq_heads, sk_pad, _candidate_tables(tn), u_all, v_all)


def _pad_tokens(a, tpad):
    return jnp.pad(a, ((0, 0), (0, tpad - a.shape[1]), (0, 0)))


def kernel(x_prompt, x_sample, cache_diff_k, cache_diff_v, cache_mem_k, cache_mem_v, state_hgrn, page_table, mem_prompt, norm_mix_g, w_in, hg_lb_logits, hg_onorm_g, da_qnorm_g, da_knorm_g, da_lambda, da_subln_g, mem_norm_g, w_mem_kv, mem_qnorm_g, mem_knorm_g, w_branch_hg, w_branch_da, w_branch_mem, w_out, norm_ffn_g, peer_wq, peer_subkeys, peer_u, peer_v):
    n_p, seq, d = x_prompt.shape
    n_s, t_new, _ = x_sample.shape
    rows_p = n_p * seq
    rows_s = n_s * t_new
    w = BRANCH_W
    depth = w_in.shape[0]

    w_in_b = w_in.astype(BF16)
    w_kv_b = w_mem_kv.astype(BF16)
    w_hg_b, w_da_b, w_cm_b, w_out_b = (a.astype(BF16) for a in (w_branch_hg, w_branch_da, w_branch_mem, w_out))
    wq_heads = peer_wq.astype(BF16).reshape(depth, d, PK_HEADS, LANES).transpose(0, 2, 1, 3)
    sk = peer_subkeys.astype(BF16)
    zk = jnp.zeros_like(sk[:, :, 0])
    sk_pad = jnp.concatenate([jnp.concatenate([sk[:, :, 0], zk], axis=-1),
                              jnp.concatenate([zk, sk[:, :, 1]], axis=-1)], axis=2)
    u_b = peer_u.astype(BF16)
    v_b = peer_v.astype(BF16)
    n_pool = cache_diff_k.shape[1]
    cache_kt = jnp.transpose(cache_diff_k, (0, 1, 3, 4, 5, 2)).reshape(depth, n_pool, w, PAGE_SIZE)
    cache_v = cache_diff_v.reshape(depth, n_pool, PAGE_SIZE * HEADS, HEAD_W)
    cmem_k = cache_mem_k.reshape(depth, n_s, MEM_LEN * HEADS, HEAD_W)
    cmem_v = cache_mem_v.reshape(depth, n_s, MEM_LEN * HEADS, HEAD_W)
    lb_logits = hg_lb_logits.astype(F32)
    rows_all = rows_p + rows_s
    with_sample = lambda full, part: lax.dynamic_update_slice(full, part, (rows_p, 0))

    mk_p, mv_p = _mem_kv(mem_prompt, mem_norm_g, w_kv_b, mem_knorm_g)

    x = jnp.concatenate([x_prompt.reshape(rows_p, d), x_sample.reshape(rows_s, d)], axis=0)
    row_mask = (np.arange(2 * HEADS * t_new)[:, None] // t_new) == (np.arange(w)[None, :] // DA_DK)
    row_mask = jnp.asarray(row_mask)
    tpad_h = 8
    tpad_c = 16

    hg_p, hg_s, dk_p, dv_p, dk_s, dv_s = [], [], [], [], [], []
    for l in range(depth):
        lam_init = 0.8 - 0.6 * math.exp(-0.3 * l)
        lam_p = da_lambda[l].astype(F32)
        proj = _norm_matmul(x, norm_mix_g[l], w_in_b[l], tn=PROJ_ROW_TILE, tc=PROJ_COL_TILE)
        qn, kn, knb, vb, qm = _prep(proj, da_qnorm_g[l], da_knorm_g[l], mem_qnorm_g[l], tn=ROW_TILE)

        hgo_p, st_p = _hgrn(proj, lb_logits, hg_onorm_g[l], None, layer=l, n_seq=n_p, seq_len=seq,
                            rows_out=rows_all, row0=0, nb=1, tc=HGRN_BLOCK, c=HGRN_CHUNK, tvalid=HGRN_CHUNK,
                            out_dtype=BF16)
        src_s = _pad_tokens(proj[rows_p:, :4 * w].reshape(n_s, t_new, 4 * w), tpad_h).reshape(n_s * tpad_h, 4 * w)
        hgo_s, st_s = _hgrn(src_s, lb_logits, hg_onorm_g[l], state_hgrn, layer=l, n_seq=n_s, seq_len=tpad_h,
                            rows_out=n_s * tpad_h, row0=0, nb=8, tc=tpad_h, c=tpad_h, tvalid=t_new, out_dtype=F32)
        hgo_s = hgo_s.reshape(n_s, tpad_h, w)[:, :t_new].reshape(rows_s, w).astype(BF16)

        da_p = _dattn_prompt(qn, knb, vb, lam_p, da_subln_g[l], n_seq=n_p, seq_len=seq, rows_out=rows_all,
                             tq=ATTN_TILE, rc=ATTN_ROW_CHUNK, lam_init=lam_init)
        qs = qn[rows_p:].reshape(n_s, t_new, w)
        qrows = jnp.where(row_mask, jnp.tile(qs, (1, 2 * HEADS, 1)), jnp.zeros((), BF16))
        da_s = _dattn_sample(page_table, qrows, knb[rows_p:].reshape(n_s, t_new, w).astype(F32),
                             vb[rows_p:].reshape(n_s, t_new, w).astype(F32), cache_kt, cache_v, lam_p,
                             da_subln_g[l], layer=l, lam_init=lam_init)
        da_s = da_s.reshape(rows_s, w).astype(BF16)

        cm_p = _cross_prompt(qm, mk_p, mv_p, layer=l, n_seq=n_p, seq_len=seq, rows_out=rows_all, tq=ATTN_TILE)
        qm_s = _pad_tokens(qm[rows_p:].reshape(n_s, t_new, w), tpad_c)
        cm_s = _cross_sample(qm_s, cmem_k, cmem_v, layer=l)[:, :t_new].reshape(rows_s, w).astype(BF16)

        x = _merge(x, proj, with_sample(hgo_p, hgo_s), with_sample(da_p, da_s), with_sample(cm_p, cm_s),
                   w_hg_b[l], w_da_b[l], w_cm_b[l], w_out_b[l], tn=ROW_TILE)

        x = _peer(x, norm_ffn_g[l], wq_heads[l], sk_pad[l], u_b, v_b, layer=l,
                  tn=PEER_TOKEN_TILE, te=PEER_EXPERT_TILE)

        hg_p.append(st_p)
        hg_s.append(st_s)
        dk_p.append(kn[:rows_p].reshape(n_p, seq, HEADS, 2, DA_DK))
        dv_p.append(proj[:rows_p, 6 * w:7 * w].reshape(n_p, seq, HEADS, HEAD_W))
        dk_s.append(kn[rows_p:].reshape(n_s, t_new, HEADS, 2, DA_DK))
        dv_s.append(proj[rows_p:, 6 * w:7 * w].reshape(n_s, t_new, HEADS, HEAD_W))

    mem_shape = (depth, n_p, MEM_LEN, HEADS, HEAD_W)
    return (x[:rows_p].reshape(n_p, seq, d), x[rows_p:].reshape(n_s, t_new, d),
            jnp.stack(hg_p), jnp.stack(hg_s), jnp.stack(dk_p), jnp.stack(dv_p),
            jnp.stack(dk_s), jnp.stack(dv_s), mk_p.reshape(mem_shape), mv_p.reshape(mem_shape))
```

```python
import functools
import math

import numpy as np
import jax
import jax.numpy as jnp
from jax import lax
from jax.experimental import pallas as pl
from jax.experimental.pallas import tpu as pltpu

F32 = jnp.float32
BF16 = jnp.bfloat16
I32 = jnp.int32

D_MODEL = 1024
DEPTH = 4
PAGE_SIZE = 128
HEADS = 4
HEAD_W = 128
BRANCH_W = HEADS * HEAD_W
DA_DK = 64
MEM_LEN = 256
N_BRANCH = 3
IN_COLS = 8 * BRANCH_W + N_BRANCH * D_MODEL
PK_HEADS = 8
N_KEYS = 128
N_EXPERTS = N_KEYS * N_KEYS
PK_TOPK = 16
EPS = 1e-6
MASK_VALUE = -1e30
NEG_INF = float("-inf")
LOG2_E = math.log2(math.e)

LANES = 128

ROW_TILE = 512
PROJ_ROW_TILE = 1536
PROJ_COL_TILE = 1024
ATTN_TILE = 1024
ATTN_ROW_CHUNK = 256
HGRN_BLOCK = 256
HGRN_CHUNK = 16
PEER_TOKEN_TILE = 256
PEER_EXPERT_TILE = 2048
NT_DIMS = (((1,), (1,)), ((), ()))
TN_DIMS = (((0,), (0,)), ((), ()))

_CAND_ROWS = 80
_ID_SENTINEL = float(1 << 20)


def _cparams(sem, vmem_mb=None):
    kw = dict(dimension_semantics=sem)
    if vmem_mb is not None:
        kw["vmem_limit_bytes"] = vmem_mb * 2 ** 20
    return pltpu.CompilerParams(**kw)


def _split_dot(x, m_bf16, terms, left=False):
    acc = None
    r = x
    for _ in range(terms):
        p = r.astype(BF16)
        r = r - p.astype(F32)
        if left:
            y = jnp.dot(m_bf16, p, preferred_element_type=F32)
        else:
            y = jnp.dot(p, m_bf16, preferred_element_type=F32)
        acc = y if acc is None else acc + y
    return acc


def _sigmoid(x):
    return 1.0 / (1.0 + jnp.exp(-x))


def _norm_matmul_kernel(x_ref, g_ref, w_ref, o_ref, hn_ref):
    @pl.when(pl.program_id(1) == 0)
    def _():
        x = x_ref[...]
        ms = jnp.mean(x * x, axis=-1, keepdims=True)
        hn_ref[...] = ((x * lax.rsqrt(ms + EPS)) * g_ref[...]).astype(BF16)

    o_ref[...] = jnp.dot(hn_ref[...], w_ref[...], preferred_element_type=F32)


def _norm_matmul(x, g, w, *, tn, tc):
    n, d = x.shape
    c = w.shape[1]
    return pl.pallas_call(
        _norm_matmul_kernel,
        grid=(n // tn, c // tc),
        in_specs=[
            pl.BlockSpec((tn, d), lambda i, j: (i, 0)),
            pl.BlockSpec((1, d), lambda i, j: (0, 0)),
            pl.BlockSpec((d, tc), lambda i, j: (0, j)),
        ],
        out_specs=pl.BlockSpec((tn, tc), lambda i, j: (i, j)),
        out_shape=jax.ShapeDtypeStruct((n, c), F32),
        scratch_shapes=[pltpu.VMEM((tn, d), BF16)],
        compiler_params=_cparams(("parallel", "arbitrary"), vmem_mb=48),
        name="norm_matmul",
    )(x, g.reshape(1, d), w)


def _group_mean_matrix(width, group):
    r = np.arange(width)
    return jnp.asarray((r[:, None] // group == r[None, :] // group) / group, BF16)


def _prep_kernel(dq_ref, dk_ref, dv_ref, mq_ref, gq_ref, gk_ref, gm_ref, bd64_ref, bd128_ref,
                 qn_ref, kn_ref, knb_ref, vb_ref, qm_ref):
    def gnorm(x, bd, g):
        ms = _split_dot(x * x, bd, 3)
        return (x * lax.rsqrt(ms + EPS)) * g

    q = gnorm(dq_ref[...], bd64_ref[...], gq_ref[...])
    qn_ref[...] = (q * (DA_DK ** -0.5 * LOG2_E)).astype(BF16)
    k = gnorm(dk_ref[...], bd64_ref[...], gk_ref[...])
    kn_ref[...] = k
    knb_ref[...] = k.astype(BF16)
    vb_ref[...] = dv_ref[...].astype(BF16)
    qm_ref[...] = gnorm(mq_ref[...], bd128_ref[...], gm_ref[...]).astype(BF16)


def _prep(proj, gq, gk, gm, *, tn):
    n = proj.shape[0]
    w = BRANCH_W
    col = lambda c: pl.BlockSpec((tn, w), lambda i, c=c: (i, c))
    const = lambda shape: pl.BlockSpec(shape, lambda i: (0, 0))
    row_out = pl.BlockSpec((tn, w), lambda i: (i, 0))
    tile = lambda g: jnp.tile(g.astype(F32), w // g.shape[0]).reshape(1, w)
    return pl.pallas_call(
        _prep_kernel,
        grid=(n // tn,),
        in_specs=[col(4), col(5), col(6), col(7), const((1, w)), const((1, w)), const((1, w)),
                  const((w, w)), const((w, w))],
        out_specs=[row_out] * 5,
        out_shape=[jax.ShapeDtypeStruct((n, w), BF16), jax.ShapeDtypeStruct((n, w), F32),
                   jax.ShapeDtypeStruct((n, w), BF16), jax.ShapeDtypeStruct((n, w), BF16),
                   jax.ShapeDtypeStruct((n, w), BF16)],
        compiler_params=_cparams(("parallel",)),
        name="qk_norm",
    )(proj, proj, proj, proj, tile(gq), tile(gk), tile(gm),
      _group_mean_matrix(w, DA_DK), _group_mean_matrix(w, HEAD_W))


def _hgrn_kernel(*refs, layer, nb, tc, c, tvalid, has_s0):
    if has_s0:
        hq_ref, hf_ref, hi_ref, hg_ref, lbl_ref, og_ref, tri_ref, s0_ref = refs[:8]
        rest = refs[8:]
    else:
        hq_ref, hf_ref, hi_ref, hg_ref, lbl_ref, og_ref, tri_ref = refs[:7]
        s0_ref = None
        rest = refs[7:]
    o_ref, sout_ref, st_ref, q_s, k_s, b_s = rest
    t = pl.program_id(1)
    rows = nb * tc

    @pl.when(t == 0)
    def _():
        if has_s0:
            for ib in range(nb):
                for h in range(HEADS):
                    st_ref[ib, h] = s0_ref[ib, h].T
        else:
            st_ref[...] = jnp.zeros_like(st_ref)

    lg = lbl_ref[...]
    e = jnp.exp(lg - jnp.max(lg, axis=0, keepdims=True))
    pr = e / jnp.sum(e, axis=0, keepdims=True)
    lb = jnp.zeros((1, BRANCH_W), F32)
    for i in range(1, layer + 1):
        lb = lb + pr[i:i + 1]
    lb = jnp.clip(lb, 0.0, 1.0 - 1e-4)

    z = hf_ref[...]
    sg = _sigmoid(z)
    logf = jnp.log(lb + (1.0 - lb) * sg)
    kk = (1.0 - lb) * _sigmoid(-z)
    if tvalid < c:
        valid = (lax.broadcasted_iota(I32, (rows, BRANCH_W), 0) % c) < tvalid
        logf = jnp.where(valid, logf, 0.0)
        kk = jnp.where(valid, kk, 0.0)
    hq = hq_ref[...]
    q_s[...] = hq * _sigmoid(hq)
    k_s[...] = kk
    b_s[...] = _split_dot(logf, tri_ref[...], 3, left=True)

    og = og_ref[...]
    row_id = lax.broadcasted_iota(I32, (c, HEAD_W), 0)
    nsub = tc // c

    def sub(i, carry):
        r0 = pl.multiple_of(i * c, c)
        ib = i // nsub
        first = (i % nsub) == 0
        new_carry = []
        for h in range(HEADS):
            sl = slice(h * HEAD_W, (h + 1) * HEAD_W)
            bg = b_s[pl.ds(r0, c), sl]
            new_carry.append(bg[c - 1:c, :])
            b = bg - jnp.where(first, 0.0, carry[h])
            q = q_s[pl.ds(r0, c), sl]
            kx = k_s[pl.ds(r0, c), sl]
            iv = hi_ref[pl.ds(r0, c), sl]
            st = st_ref[ib, h]
            o_inter = lax.dot_general((q * jnp.exp(b)).astype(BF16), st.astype(BF16), NT_DIMS,
                                      preferred_element_type=F32)
            o_rows = []
            for tt in range(c):
                d = b[tt:tt + 1, :] - b
                w = jnp.exp(jnp.where(row_id <= tt, d, MASK_VALUE))
                p = (q[tt:tt + 1, :] * kx) * w
                a = jnp.sum(p, axis=-1, keepdims=True)
                o_rows.append(jnp.sum(a * iv, axis=0, keepdims=True))
            o = o_inter + jnp.concatenate(o_rows, axis=0)
            bl = b[c - 1:c, :]
            kd = kx * jnp.exp(bl - b)
            upd = lax.dot_general(iv.astype(BF16), kd.astype(BF16), TN_DIMS, preferred_element_type=F32)
            st_ref[ib, h] = jnp.exp(bl) * st + upd
            ms = jnp.mean(o * o, axis=-1, keepdims=True)
            hg = hg_ref[pl.ds(r0, c), sl]
            o_ref[pl.ds(r0, c), sl] = (((o * lax.rsqrt(ms + EPS)) * og) * (hg * _sigmoid(hg))).astype(o_ref.dtype)
        return tuple(new_carry)

    lax.fori_loop(0, nb * nsub, sub, tuple(jnp.zeros((1, HEAD_W), F32) for _ in range(HEADS)))

    @pl.when(t == pl.num_programs(1) - 1)
    def _():
        for ib in range(nb):
            for h in range(HEADS):
                sout_ref[ib, h] = st_ref[ib, h].T


def _hgrn(src, lb_logits, onorm_g, s0_all, *, layer, n_seq, seq_len, rows_out, row0, nb, tc, c, tvalid,
          out_dtype):
    rows = nb * tc
    nt = seq_len // tc
    rb0 = row0 // rows
    w = BRANCH_W
    col = lambda cc: pl.BlockSpec((rows, w), lambda b, t, cc=cc: (rb0 + b * nt + t, cc))
    r = np.arange(rows)
    tri = jnp.asarray((r[:, None] >= r[None, :]) & (r[:, None] // tc == r[None, :] // tc), BF16)
    in_specs = [col(0), col(1), col(2), col(3),
                pl.BlockSpec((DEPTH, w), lambda b, t: (0, 0)),
                pl.BlockSpec((1, HEAD_W), lambda b, t: (0, 0)),
                pl.BlockSpec((rows, rows), lambda b, t: (0, 0))]
    args = [src, src, src, src, lb_logits, onorm_g.reshape(1, HEAD_W), tri]
    has_s0 = s0_all is not None
    if has_s0:
        in_specs.append(pl.BlockSpec((None, nb, HEADS, HEAD_W, HEAD_W), lambda b, t: (layer, b, 0, 0, 0)))
        args.append(s0_all)
    kern = functools.partial(_hgrn_kernel, layer=layer, nb=nb, tc=tc, c=c, tvalid=tvalid, has_s0=has_s0)
    return pl.pallas_call(
        kern,
        grid=(n_seq // nb, nt),
        in_specs=in_specs,
        out_specs=[pl.BlockSpec((rows, w), lambda b, t: (b * nt + t, 0)),
                   pl.BlockSpec((nb, HEADS, HEAD_W, HEAD_W), lambda b, t: (b, 0, 0, 0))],
        out_shape=[jax.ShapeDtypeStruct((rows_out, w), out_dtype),
                   jax.ShapeDtypeStruct((n_seq, HEADS, HEAD_W, HEAD_W), F32)],
        scratch_shapes=[pltpu.VMEM((nb, HEADS, HEAD_W, HEAD_W), F32),
                        pltpu.VMEM((rows, w), F32), pltpu.VMEM((rows, w), F32), pltpu.VMEM((rows, w), F32)],
        compiler_params=_cparams(("parallel", "arbitrary")),
        name="hgrn2",
    )(*args)


def _lambda_value(lp, lam_init):
    a = jnp.sum(lp[0:1] * lp[1:2], axis=-1, keepdims=True)
    b = jnp.sum(lp[2:3] * lp[3:4], axis=-1, keepdims=True)
    return jnp.exp(a) - jnp.exp(b) + lam_init


def _dattn_prompt_kernel(iq_tab, ik_tab, lam_ref, q_ref, k_ref, v_ref, sg_ref, o_ref, q2_s, m_s, acc_s,
                         *, tq, rc, lam_init):
    step = pl.program_id(2)
    iq = iq_tab[step]
    ik = ik_tab[step]

    @pl.when(ik == 0)
    def _():
        q = q_ref[...].astype(F32)
        lane = lax.broadcasted_iota(I32, q.shape, 1)
        q2_s[0:tq, :] = jnp.where(lane < DA_DK, q, 0.0).astype(BF16)
        q2_s[tq:2 * tq, :] = jnp.where(lane >= DA_DK, q, 0.0).astype(BF16)
        m_s[...] = jnp.full_like(m_s, NEG_INF)
        acc_s[...] = jnp.zeros_like(acc_s)

    def update(masked):
        v1 = jnp.concatenate([v_ref[...], jnp.ones((tq, LANES), BF16)], axis=1)
        for c0 in range(0, 2 * tq, rc):
            rows = slice(c0, c0 + rc)
            nk = (c0 % tq) + rc if masked else tq
            s = lax.dot_general(q2_s[rows, :], k_ref[0:nk, :], NT_DIMS, preferred_element_type=F32)
            if masked:
                r = lax.broadcasted_iota(I32, s.shape, 0) + (c0 % tq)
                cidx = lax.broadcasted_iota(I32, s.shape, 1)
                s = jnp.where(cidx <= r, s, MASK_VALUE)
            m_prev = m_s[rows, :]
            m_new = jnp.maximum(m_prev, jnp.max(s, axis=-1, keepdims=True))
            alpha = jnp.exp2(m_prev - m_new)
            p = jnp.exp2(s - jnp.tile(m_new, (1, nk // LANES)))
            acc_s[rows, :] = (jnp.tile(alpha, (1, 2)) * acc_s[rows, :]
                              + jnp.dot(p.astype(BF16), v1[0:nk, :], preferred_element_type=F32))
            m_s[rows, :] = m_new

    @pl.when(ik < iq)
    def _():
        update(False)

    @pl.when(ik == iq)
    def _():
        update(True)
        lam = _lambda_value(lam_ref[...], lam_init)
        o0 = acc_s[0:tq, 0:HEAD_W] / acc_s[0:tq, HEAD_W:]
        o1 = acc_s[tq:2 * tq, 0:HEAD_W] / acc_s[tq:2 * tq, HEAD_W:]
        da = o0 - lam * o1
        ms = jnp.mean(da * da, axis=-1, keepdims=True)
        o_ref[...] = (((da * lax.rsqrt(ms + EPS)) * sg_ref[...]) * (1.0 - lam_init)).astype(o_ref.dtype)


def _dattn_prompt(qn, knb, vb, lam_p, subln_g, *, n_seq, seq_len, rows_out, tq, rc, lam_init):
    nq = seq_len // tq
    pairs = [(iq, ik) for iq in range(nq) for ik in range(iq + 1)]
    iq_tab = jnp.asarray([p[0] for p in pairs], I32)
    ik_tab = jnp.asarray([p[1] for p in pairs], I32)
    qspec = pl.BlockSpec((tq, HEAD_W), lambda b, h, s, iqt, ikt: (b * nq + iqt[s], h))
    kspec = pl.BlockSpec((tq, HEAD_W), lambda b, h, s, iqt, ikt: (b * nq + ikt[s], h))
    grid_spec = pltpu.PrefetchScalarGridSpec(
        num_scalar_prefetch=2,
        grid=(n_seq, HEADS, len(pairs)),
        in_specs=[pl.BlockSpec((4, DA_DK), lambda b, h, s, iqt, ikt: (0, 0)), qspec, kspec, kspec,
                  pl.BlockSpec((1, HEAD_W), lambda b, h, s, iqt, ikt: (0, 0))],
        out_specs=qspec,
        scratch_shapes=[pltpu.VMEM((2 * tq, HEAD_W), BF16), pltpu.VMEM((2 * tq, LANES), F32),
                        pltpu.VMEM((2 * tq, HEAD_W + LANES), F32)],
    )
    return pl.pallas_call(
        functools.partial(_dattn_prompt_kernel, tq=tq, rc=rc, lam_init=lam_init),
        grid_spec=grid_spec,
        out_shape=jax.ShapeDtypeStruct((rows_out, BRANCH_W), BF16),
        compiler_params=_cparams(("parallel", "parallel", "arbitrary")),
        name="diff_attn_prompt",
    )(iq_tab, ik_tab, lam_p, qn, knb, vb, subln_g.reshape(1, HEAD_W))


def _dattn_sample_kernel(pt_ref, lam_ref, qr_ref, kc_ref, vc_ref, sg_ref, *rest, n_pages, t_new, lam_init):
    k_refs = rest[:n_pages]
    v_refs = rest[n_pages:2 * n_pages]
    o_ref = rest[2 * n_pages]
    nrow = 2 * HEADS * t_new
    qr = qr_ref[...]
    s_past = jnp.concatenate(
        [jnp.dot(qr, k[...].astype(BF16), preferred_element_type=F32) for k in k_refs], axis=1)

    qf = qr.astype(F32)
    kc = kc_ref[...]
    vc = vc_ref[...]
    t_of_row = lax.broadcasted_iota(I32, (nrow, 1), 0) % t_new
    s_new = []
    for j in range(t_new):
        sj = jnp.sum(qf * kc[j:j + 1, :], axis=-1, keepdims=True)
        s_new.append(jnp.where(t_of_row >= j, sj, MASK_VALUE))
    m = jnp.max(s_past, axis=-1, keepdims=True)
    for sj in s_new:
        m = jnp.maximum(m, sj)
    p_past = jnp.exp2(s_past - m)
    p_new = [jnp.exp2(sj - m) for sj in s_new]
    l = jnp.sum(p_past, axis=-1, keepdims=True)
    for pj in p_new:
        l = l + pj

    lam = _lambda_value(lam_ref[...], lam_init)
    rows_h = 2 * t_new
    for h in range(HEADS):
        sl = slice(h * HEAD_W, (h + 1) * HEAD_W)
        r0 = h * rows_h
        v_h = jnp.concatenate([v[pl.ds(h, PAGE_SIZE, stride=HEADS), :] for v in v_refs], axis=0)
        acc = jnp.dot(p_past[r0:r0 + rows_h].astype(BF16), v_h.astype(BF16), preferred_element_type=F32)
        for j in range(t_new):
            acc = acc + p_new[j][r0:r0 + rows_h] * vc[j:j + 1, sl]
        on = acc / l[r0:r0 + rows_h]
        da = on[:t_new] - lam * on[t_new:]
        ms = jnp.mean(da * da, axis=-1, keepdims=True)
        o_ref[:, sl] = (((da * lax.rsqrt(ms + EPS)) * sg_ref[...]) * (1.0 - lam_init)).astype(o_ref.dtype)


def _dattn_sample(page_table, qrows, kcur, vcur, cache_kt, cache_v, lam_p, subln_g, *, layer, lam_init):
    n_seq, n_pages = page_table.shape
    t_new = kcur.shape[1]
    nrow = qrows.shape[1]
    per_seq = lambda shape: pl.BlockSpec((None,) + shape, lambda b, pt: (b, 0, 0))
    page = lambda p: pl.BlockSpec((None, None, PAGE_SIZE * HEADS, HEAD_W),
                                  lambda b, pt, p=p: (layer, pt[b * n_pages + p], 0, 0))
    grid_spec = pltpu.PrefetchScalarGridSpec(
        num_scalar_prefetch=1,
        grid=(n_seq,),
        in_specs=[pl.BlockSpec((4, DA_DK), lambda b, pt: (0, 0)),
                  per_seq((nrow, BRANCH_W)), per_seq((t_new, BRANCH_W)), per_seq((t_new, BRANCH_W)),
                  pl.BlockSpec((1, HEAD_W), lambda b, pt: (0, 0))]
                 + [page(p) for p in range(n_pages)] * 2,
        out_specs=per_seq((t_new, BRANCH_W)),
    )
    return pl.pallas_call(
        functools.partial(_dattn_sample_kernel, n_pages=n_pages, t_new=t_new, lam_init=lam_init),
        grid_spec=grid_spec,
        out_shape=jax.ShapeDtypeStruct((n_seq, t_new, BRANCH_W), F32),
        compiler_params=_cparams(("parallel",), vmem_mb=40),
        name="diff_attn_sample",
    )(page_table.reshape(-1), lam_p, qrows, kcur, vcur, subln_g.reshape(1, HEAD_W),
      *([cache_kt] * n_pages), *([cache_v] * n_pages))


def _mem_kv_kernel(m_ref, g_ref, w_ref, kg_ref, bd_ref, mk_ref, mv_ref):
    x = m_ref[...]
    ms = jnp.mean(x * x, axis=-1, keepdims=True)
    hn = ((x * lax.rsqrt(ms + EPS)) * g_ref[...]).astype(BF16)
    kv = jnp.dot(hn, w_ref[...], preferred_element_type=F32)
    k = kv[:, :BRANCH_W]
    ms = _split_dot(k * k, bd_ref[...], 3)
    mk_ref[...] = (k * lax.rsqrt(ms + EPS)) * kg_ref[...]
    mv_ref[...] = kv[:, BRANCH_W:]


def _mem_kv(mem, norm_g, w_kv, knorm_g):
    n_seq = mem.shape[0]
    w = BRANCH_W
    out = pl.BlockSpec((None, None, MEM_LEN, w), lambda l, b: (l, b, 0, 0))
    return pl.pallas_call(
        _mem_kv_kernel,
        grid=(DEPTH, n_seq),
        in_specs=[pl.BlockSpec((None, MEM_LEN, D_MODEL), lambda l, b: (b, 0, 0)),
                  pl.BlockSpec((None, 1, D_MODEL), lambda l, b: (l, 0, 0)),
                  pl.BlockSpec((None, D_MODEL, 2 * w), lambda l, b: (l, 0, 0)),
                  pl.BlockSpec((None, 1, w), lambda l, b: (l, 0, 0)),
                  pl.BlockSpec((w, w), lambda l, b: (0, 0))],
        out_specs=[out, out],
        out_shape=[jax.ShapeDtypeStruct((DEPTH, n_seq, MEM_LEN, w), F32)] * 2,
        compiler_params=_cparams(("parallel", "parallel")),
        name="mem_kv",
    )(mem, norm_g.reshape(DEPTH, 1, D_MODEL), w_kv,
      jnp.tile(knorm_g.astype(F32), (1, HEADS)).reshape(DEPTH, 1, w), _group_mean_matrix(w, HEAD_W))


def _cross_kernel(q_ref, mk_ref, mv_ref, o_ref, *, head_rows):
    q = q_ref[...]
    for h in range(HEADS):
        sl = slice(h * HEAD_W, (h + 1) * HEAD_W)
        if head_rows:
            mk = mk_ref[pl.ds(h, MEM_LEN, stride=HEADS), :].astype(BF16)
            mv = mv_ref[pl.ds(h, MEM_LEN, stride=HEADS), :].astype(BF16)
        else:
            mk = mk_ref[:, sl].astype(BF16)
            mv = mv_ref[:, sl].astype(BF16)
        s = lax.dot_general(q[:, sl], mk, NT_DIMS, preferred_element_type=F32) * (HEAD_W ** -0.5)
        p = jnp.exp(s - jnp.max(s, axis=-1, keepdims=True))
        l = jnp.sum(p, axis=-1, keepdims=True)
        o = jnp.dot(p.astype(BF16), mv, preferred_element_type=F32) / l
        o_ref[:, sl] = o.astype(o_ref.dtype)


def _cross_prompt(qm, mk_all, mv_all, *, layer, n_seq, seq_len, rows_out, tq):
    nq = seq_len // tq
    mem = pl.BlockSpec((None, None, MEM_LEN, BRANCH_W), lambda b, i: (layer, b, 0, 0))
    row = pl.BlockSpec((tq, BRANCH_W), lambda b, i: (b * nq + i, 0))
    return pl.pallas_call(
        functools.partial(_cross_kernel, head_rows=False),
        grid=(n_seq, nq),
        in_specs=[row, mem, mem],
        out_specs=row,
        out_shape=jax.ShapeDtypeStruct((rows_out, BRANCH_W), BF16),
        compiler_params=_cparams(("parallel", "parallel")),
        name="cross_attn_prompt",
    )(qm, mk_all, mv_all)


def _cross_sample(qm_pad, mk_all, mv_all, *, layer):
    n_seq, tpad, w = qm_pad.shape
    mem = pl.BlockSpec((None, None, MEM_LEN * HEADS, HEAD_W), lambda b: (layer, b, 0, 0))
    row = pl.BlockSpec((None, tpad, w), lambda b: (b, 0, 0))
    return pl.pallas_call(
        functools.partial(_cross_kernel, head_rows=True),
        grid=(n_seq,),
        in_specs=[row, mem, mem],
        out_specs=row,
        out_shape=jax.ShapeDtypeStruct((n_seq, tpad, w), F32),
        compiler_params=_cparams(("parallel",)),
        name="cross_attn_sample",
    )(qm_pad, mk_all, mv_all)


def _merge_kernel(x_ref, g0_ref, g1_ref, g2_ref, a0_ref, a1_ref, a2_ref, w0_ref, w1_ref, w2_ref, wo_ref, o_ref):
    merged = None
    for g_ref, a_ref, w_ref in ((g0_ref, a0_ref, w0_ref), (g1_ref, a1_ref, w1_ref), (g2_ref, a2_ref, w2_ref)):
        y = _sigmoid(g_ref[...]) * jnp.dot(a_ref[...], w_ref[...], preferred_element_type=F32)
        merged = y if merged is None else merged + y
    o_ref[...] = x_ref[...] + jnp.dot(merged.astype(BF16), wo_ref[...], preferred_element_type=F32)


def _merge(x, proj, hg, da, cm, w_hg, w_da, w_mem, w_out, *, tn):
    n, d = x.shape
    w = BRANCH_W
    gate_col0 = (8 * w) // d
    row = lambda width: pl.BlockSpec((tn, width), lambda i: (i, 0))
    gate = lambda c: pl.BlockSpec((tn, d), lambda i, c=c: (i, gate_col0 + c))
    const = lambda shape: pl.BlockSpec(shape, lambda i: (0, 0))
    return pl.pallas_call(
        _merge_kernel,
        grid=(n // tn,),
        in_specs=[row(d), gate(0), gate(1), gate(2), row(w), row(w), row(w),
                  const((w, d)), const((w, d)), const((w, d)), const((d, d))],
        out_specs=row(d),
        out_shape=jax.ShapeDtypeStruct((n, d), F32),
        compiler_params=_cparams(("parallel",), vmem_mb=48),
        name="branch_merge",
    )(x, proj, proj, proj, hg, da, cm, w_hg, w_da, w_mem, w_out)


def _topk_rows(scores, k, key_id):
    scores = list(scores)
    vals = [[] for _ in scores]
    ids = [[] for _ in scores]
    for _ in range(k):
        for n, s in enumerate(scores):
            m = jnp.max(s, axis=0, keepdims=True)
            sel = jnp.min(jnp.where(s == m, key_id, _ID_SENTINEL), axis=0, keepdims=True)
            scores[n] = jnp.where(key_id == sel, NEG_INF, s)
            vals[n].append(m)
            ids[n].append(sel)
        yield None
    return vals, ids


def _candidate_tables(tt):
    flat = np.full((_CAND_ROWS,), _ID_SENTINEL, np.float64)
    flat[0:16] = np.arange(16)
    flat[16:24] = 16 + np.arange(8)
    for j1 in range(2, 8):
        lim = PK_TOPK // (j1 + 1)
        flat[24 + (j1 - 2) * 8:24 + (j1 - 2) * 8 + lim] = j1 * 16 + np.arange(lim)
    flat[72:80] = (8 + np.arange(8)) * 16
    return jnp.asarray(np.broadcast_to(flat[:, None], (_CAND_ROWS, tt)), F32)


def _norm_bf16(x, g):
    ms = jnp.mean(x * x, axis=-1, keepdims=True)
    return ((x * lax.rsqrt(ms + EPS)) * g).astype(BF16)


def _select_head_steps(h, hn, wq_ref, sk_ref, cflat, eid_t, gate_t, lane0=0):
    tt = hn.shape[0]
    cflat = cflat[:, :tt]
    key_id = lax.broadcasted_iota(I32, (N_KEYS, tt), 0).astype(F32)
    cvalid = cflat < float(PK_TOPK * PK_TOPK)
    qh = jnp.dot(hn, wq_ref[h], preferred_element_type=F32).astype(BF16)
    sc = lax.dot_general(sk_ref[h], qh, NT_DIMS, preferred_element_type=F32)
    yield None
    (v1, v2), (i1, i2) = yield from _topk_rows([sc[:N_KEYS], sc[N_KEYS:]], PK_TOPK, key_id)
    v2a = jnp.concatenate(v2, axis=0)
    i2a = jnp.concatenate(i2, axis=0)
    pieces_v = [v1[0] + v2a, v1[1] + v2a[:8]]
    pieces_e = [i1[0] * N_KEYS + i2a, i1[1] * N_KEYS + i2a[:8]]
    for j1 in range(2, 8):
        pieces_v.append(v1[j1] + v2a[:8])
        pieces_e.append(i1[j1] * N_KEYS + i2a[:8])
    pieces_v.append(jnp.concatenate(v1[8:], axis=0) + v2a[:1])
    pieces_e.append(jnp.concatenate(i1[8:], axis=0) * N_KEYS + i2a[:1])
    cand = jnp.where(cvalid, jnp.concatenate(pieces_v, axis=0), NEG_INF)
    ce = jnp.concatenate(pieces_e, axis=0)
    fv, fe = [], []
    for _ in range(PK_TOPK):
        m = jnp.max(cand, axis=0, keepdims=True)
        sel = jnp.min(jnp.where(cand == m, cflat, _ID_SENTINEL), axis=0, keepdims=True)
        hit = cflat == sel
        fe.append(jnp.max(jnp.where(hit, ce, -1.0), axis=0, keepdims=True))
        cand = jnp.where(hit, NEG_INF, cand)
        fv.append(m)
        yield None
    fva = jnp.concatenate(fv, axis=0)
    ex = jnp.exp(fva - fva[0:1])
    r0 = pl.multiple_of(h * PK_TOPK, PK_TOPK)
    gate_t[pl.ds(r0, PK_TOPK), lane0:lane0 + tt] = ex / jnp.sum(ex, axis=0, keepdims=True)
    eid_t[pl.ds(r0, PK_TOPK), lane0:lane0 + tt] = jnp.concatenate(fe, axis=0)


def _select_head(*args):
    for _ in _select_head_steps(*args):
        pass


def _build_routing(eid_s, gate_s, w_s, *, tn, pitch):
    n_sel = PK_HEADS * PK_TOPK
    row_id = lax.broadcasted_iota(I32, (N_KEYS, n_sel), 0).astype(F32).astype(BF16)
    one = jnp.ones((N_KEYS, n_sel), BF16)
    zero = jnp.zeros((N_KEYS, n_sel), BF16)
    bf16_rows = 16

    def all_rows(v):
        return jnp.tile(jnp.broadcast_to(v, (bf16_rows, n_sel)).astype(BF16), (N_KEYS // bf16_rows, 1))

    group = 32

    def build(n8, carry):
        n0 = pl.multiple_of(n8 * group, group)
        e8 = eid_s[pl.ds(n0, group), :]
        g8 = gate_s[pl.ds(n0, group), :]
        for r in range(group):
            e = e8[r:r + 1, :]
            g = 0.5 * g8[r:r + 1, :]
            a_id = jnp.floor(e * (1.0 / N_KEYS))
            sel_a = jnp.where(row_id == all_rows(a_id), one, zero)
            hit_b = row_id == all_rows(e - a_id * N_KEYS)
            g_hi = g.astype(BF16).astype(F32)
            g_lo = g - g_hi
            lhs = jnp.concatenate([sel_a, sel_a], axis=1)
            rhs = jnp.concatenate([jnp.where(hit_b, all_rows(g_hi), zero),
                                   jnp.where(hit_b, all_rows(g_lo), zero)], axis=1)
            w = lax.dot_general(lhs, rhs, NT_DIMS, preferred_element_type=F32)
            w_s[pl.ds(n0 + r, N_KEYS, stride=pitch), :] = w
        return carry

    lax.fori_loop(0, tn // group, build, 0)


def _peer_kernel(xc_ref, xn_ref, g_ref, wq_ref, sk_ref, cflat_ref, u_ref, v_ref, o_ref,
                 w_s, acc_s, hn_s, eid_t, gate_t, eid_s, gate_s, *, tn, te, pitch):
    i = pl.program_id(0)
    j = pl.program_id(1)
    cur = i % 2
    nxt = 1 - cur
    cflat = cflat_ref[...]

    @pl.when(j == 0)
    def _():
        @pl.when(i == 0)
        def _():
            hn0 = _norm_bf16(xc_ref[...], g_ref[...])
            hn_s[0] = hn0

            def head(h, carry):
                _select_head(h, hn0, wq_ref, sk_ref, cflat, eid_t, gate_t)
                return carry

            lax.fori_loop(0, PK_HEADS, head, 0)
            eid_s[...] = eid_t[...].T
            gate_s[...] = gate_t[...].T

        _build_routing(eid_s, gate_s, w_s, tn=tn, pitch=pitch)
        acc_s[...] = jnp.zeros_like(acc_s)
        hn_s[nxt] = _norm_bf16(xn_ref[...], g_ref[...])

    sub = LANES
    hn_next = hn_s[nxt]
    selects = [_select_head_steps(j, hn_next[t0:t0 + sub], wq_ref, sk_ref, cflat, eid_t, gate_t, lane0=t0)
               for t0 in range(0, tn, sub)]

    def advance(units):
        for _ in range(units):
            while selects and next(selects[0], "done") == "done":
                selects.pop(0)
                if not selects:
                    return

    advance(1)
    units_per_piece = 8
    hn = hn_s[cur]
    acc = acc_s[...]
    e_chunk = 2 * LANES
    v_chunk = 8 * e_chunk
    for c0 in range(0, te, v_chunk):
        pieces = []
        for c1 in range(c0, c0 + v_chunk, e_chunk):
            a = lax.dot_general(hn, u_ref[c1:c1 + e_chunk, :], NT_DIMS, preferred_element_type=F32)
            advance(units_per_piece // 2)
            for r in range(e_chunk // LANES):
                ar = a[:, r * LANES:(r + 1) * LANES]
                a_id = j * (te // LANES) + c1 // LANES + r
                wr = w_s[pl.ds(pl.multiple_of(a_id * pitch, 8), tn), :]
                act = ar * (1.0 + lax.erf(ar * (0.5 ** 0.5)))
                pieces.append((wr * act).astype(BF16))
            advance(units_per_piece - units_per_piece // 2)
        acc = acc + jnp.dot(jnp.concatenate(pieces, axis=1), v_ref[c0:c0 + v_chunk, :],
                            preferred_element_type=F32)
        advance(units_per_piece)
    acc_s[...] = acc
    advance(len(selects) * 64)

    @pl.when(j == pl.num_programs(1) - 1)
    def _():
        o_ref[...] = xc_ref[...] + acc_s[...]
        eid_s[...] = eid_t[...].T
        gate_s[...] = gate_t[...].T


def _peer(x, g, wq_heads, sk_pad, u_all, v_all, *, layer, tn, te):
    n, d = x.shape
    nt = n // tn
    assert N_EXPERTS // te == PK_HEADS
    sel = PK_HEADS * PK_TOPK
    pitch = tn + 8
    table = pl.BlockSpec((None, te, d), lambda i, j: (layer, j, 0))
    const = lambda shape: pl.BlockSpec(shape, lambda i, j: (0,) * len(shape))
    return pl.pallas_call(
        functools.partial(_peer_kernel, tn=tn, te=te, pitch=pitch),
        grid=(nt, N_EXPERTS // te),
        in_specs=[pl.BlockSpec((tn, d), lambda i, j: (i, 0)),
                  pl.BlockSpec((tn, d), lambda i, j: (jnp.minimum(i + 1, nt - 1), 0)),
                  const((1, d)), const((PK_HEADS, d, LANES)), const((PK_HEADS, 2 * N_KEYS, LANES)),
                  const((_CAND_ROWS, tn)), table, table],
        out_specs=pl.BlockSpec((tn, d), lambda i, j: (i, 0)),
        out_shape=jax.ShapeDtypeStruct((n, d), F32),
        scratch_shapes=[pltpu.VMEM((N_KEYS * pitch, LANES), F32), pltpu.VMEM((tn, d), F32),
                        pltpu.VMEM((2, tn, d), BF16),
                        pltpu.VMEM((sel, tn), F32), pltpu.VMEM((sel, tn), F32),
                        pltpu.VMEM((tn, sel), F32), pltpu.VMEM((tn, sel), F32)],
        compiler_params=_cparams(("arbitrary", "arbitrary"), vmem_mb=60),
        name="peer",
    )(x, x, g.reshape(1, d), wq_heads, sk_pad, _candidate_tables(tn), u_all, v_all)


def _pad_tokens(a, tpad):
    return jnp.pad(a, ((0, 0), (0, tpad - a.shape[1]), (0, 0)))


def kernel(x_prompt, x_sample, cache_diff_k, cache_diff_v, cache_mem_k, cache_mem_v, state_hgrn, page_table, mem_prompt, norm_mix_g, w_in, hg_lb_logits, hg_onorm_g, da_qnorm_g, da_knorm_g, da_lambda, da_subln_g, mem_norm_g, w_mem_kv, mem_qnorm_g, mem_knorm_g, w_branch_hg, w_branch_da, w_branch_mem, w_out, norm_ffn_g, peer_wq, peer_subkeys, peer_u, peer_v):
    n_p, seq, d = x_prompt.shape
    n_s, t_new, _ = x_sample.shape
    rows_p = n_p * seq
    rows_s = n_s * t_new
    w = BRANCH_W
    depth = w_in.shape[0]

    w_in_b = w_in.astype(BF16)
    w_kv_b = w_mem_kv.astype(BF16)
    w_hg_b, w_da_b, w_cm_b, w_out_b = (a.astype(BF16) for a in (w_branch_hg, w_branch_da, w_branch_mem, w_out))
    wq_heads = peer_wq.astype(BF16).reshape(depth, d, PK_HEADS, LANES).transpose(0, 2, 1, 3)
    sk = peer_subkeys.astype(BF16)
    zk = jnp.zeros_like(sk[:, :, 0])
    sk_pad = jnp.concatenate([jnp.concatenate([sk[:, :, 0], zk], axis=-1),
                              jnp.concatenate([zk, sk[:, :, 1]], axis=-1)], axis=2)
    u_b = peer_u.astype(BF16)
    v_b = peer_v.astype(BF16)
    n_pool = cache_diff_k.shape[1]
    cache_kt = jnp.transpose(cache_diff_k, (0, 1, 3, 4, 5, 2)).reshape(depth, n_pool, w, PAGE_SIZE)
    cache_v = cache_diff_v.reshape(depth, n_pool, PAGE_SIZE * HEADS, HEAD_W)
    cmem_k = cache_mem_k.reshape(depth, n_s, MEM_LEN * HEADS, HEAD_W)
    cmem_v = cache_mem_v.reshape(depth, n_s, MEM_LEN * HEADS, HEAD_W)
    lb_logits = hg_lb_logits.astype(F32)
    rows_all = rows_p + rows_s
    with_sample = lambda full, part: lax.dynamic_update_slice(full, part, (rows_p, 0))

    mk_p, mv_p = _mem_kv(mem_prompt, mem_norm_g, w_kv_b, mem_knorm_g)

    x = jnp.concatenate([x_prompt.reshape(rows_p, d), x_sample.reshape(rows_s, d)], axis=0)
    row_mask = (np.arange(2 * HEADS * t_new)[:, None] // t_new) == (np.arange(w)[None, :] // DA_DK)
    row_mask = jnp.asarray(row_mask)
    tpad_h = 8
    tpad_c = 16

    hg_p, hg_s, dk_p, dv_p, dk_s, dv_s = [], [], [], [], [], []
    for l in range(depth):
        lam_init = 0.8 - 0.6 * math.exp(-0.3 * l)
        lam_p = da_lambda[l].astype(F32)
        proj = _norm_matmul(x, norm_mix_g[l], w_in_b[l], tn=PROJ_ROW_TILE, tc=PROJ_COL_TILE)
        qn, kn, knb, vb, qm = _prep(proj, da_qnorm_g[l], da_knorm_g[l], mem_qnorm_g[l], tn=ROW_TILE)

        hgo_p, st_p = _hgrn(proj, lb_logits, hg_onorm_g[l], None, layer=l, n_seq=n_p, seq_len=seq,
                            rows_out=rows_all, row0=0, nb=1, tc=HGRN_BLOCK, c=HGRN_CHUNK, tvalid=HGRN_CHUNK,
                            out_dtype=BF16)
        src_s = _pad_tokens(proj[rows_p:, :4 * w].reshape(n_s, t_new, 4 * w), tpad_h).reshape(n_s * tpad_h, 4 * w)
        hgo_s, st_s = _hgrn(src_s, lb_logits, hg_onorm_g[l], state_hgrn, layer=l, n_seq=n_s, seq_len=tpad_h,
                            rows_out=n_s * tpad_h, row0=0, nb=8, tc=tpad_h, c=tpad_h, tvalid=t_new, out_dtype=F32)
        hgo_s = hgo_s.reshape(n_s, tpad_h, w)[:, :t_new].reshape(rows_s, w).astype(BF16)

        da_p = _dattn_prompt(qn, knb, vb, lam_p, da_subln_g[l], n_seq=n_p, seq_len=seq, rows_out=rows_all,
                             tq=ATTN_TILE, rc=ATTN_ROW_CHUNK, lam_init=lam_init)
        qs = qn[rows_p:].reshape(n_s, t_new, w)
        qrows = jnp.where(row_mask, jnp.tile(qs, (1, 2 * HEADS, 1)), jnp.zeros((), BF16))
        da_s = _dattn_sample(page_table, qrows, knb[rows_p:].reshape(n_s, t_new, w).astype(F32),
                             vb[rows_p:].reshape(n_s, t_new, w).astype(F32), cache_kt, cache_v, lam_p,
                             da_subln_g[l], layer=l, lam_init=lam_init)
        da_s = da_s.reshape(rows_s, w).astype(BF16)

        cm_p = _cross_prompt(qm, mk_p, mv_p, layer=l, n_seq=n_p, seq_len=seq, rows_out=rows_all, tq=ATTN_TILE)
        qm_s = _pad_tokens(qm[rows_p:].reshape(n_s, t_new, w), tpad_c)
        cm_s = _cross_sample(qm_s, cmem_k, cmem_v, layer=l)[:, :t_new].reshape(rows_s, w).astype(BF16)

        x = _merge(x, proj, with_sample(hgo_p, hgo_s), with_sample(da_p, da_s), with_sample(cm_p, cm_s),
                   w_hg_b[l], w_da_b[l], w_cm_b[l], w_out_b[l], tn=ROW_TILE)

        x = _peer(x, norm_ffn_g[l], wq_heads[l], sk_pad[l], u_b, v_b, layer=l,
                  tn=PEER_TOKEN_TILE, te=PEER_EXPERT_TILE)

        hg_p.append(st_p)
        hg_s.append(st_s)
        dk_p.append(kn[:rows_p].reshape(n_p, seq, HEADS, 2, DA_DK))
        dv_p.append(proj[:rows_p, 6 * w:7 * w].reshape(n_p, seq, HEADS, HEAD_W))
        dk_s.append(kn[rows_p:].reshape(n_s, t_new, HEADS, 2, DA_DK))
        dv_s.append(proj[rows_p:, 6 * w:7 * w].reshape(n_s, t_new, HEADS, HEAD_W))

    mem_shape = (depth, n_p, MEM_LEN, HEADS, HEAD_W)
    return (x[:rows_p].reshape(n_p, seq, d), x[rows_p:].reshape(n_s, t_new, d),
            jnp.stack(hg_p), jnp.stack(hg_s), jnp.stack(dk_p), jnp.stack(dv_p),
            jnp.stack(dk_s), jnp.stack(dv_s), mk_p.reshape(mem_shape), mv_p.reshape(mem_shape))
```

```python
import functools
import math

import numpy as np
import jax
import jax.numpy as jnp
from jax import lax
from jax.experimental import pallas as pl
from jax.experimental.pallas import tpu as pltpu

F32 = jnp.float32
BF16 = jnp.bfloat16
I32 = jnp.int32

D_MODEL = 1024
DEPTH = 4
PAGE_SIZE = 128
HEADS = 4
HEAD_W = 128
BRANCH_W = HEADS * HEAD_W
DA_DK = 64
MEM_LEN = 256
N_BRANCH = 3
IN_COLS = 8 * BRANCH_W + N_BRANCH * D_MODEL
PK_HEADS = 8
N_KEYS = 128
N_EXPERTS = N_KEYS * N_KEYS
PK_TOPK = 16
EPS = 1e-6
MASK_VALUE = -1e30
NEG_INF = float("-inf")
LOG2_E = math.log2(math.e)

LANES = 128

ROW_TILE = 512
PROJ_ROW_TILE = 1536
PROJ_COL_TILE = 1024
ATTN_TILE = 1024
ATTN_ROW_CHUNK = 512
HGRN_BLOCK = 256
HGRN_CHUNK = 16
CROSS_SAMPLE_SEQS = 2
PEER_TOKEN_TILE = 256
PEER_EXPERT_TILE = 2048
NT_DIMS = (((1,), (1,)), ((), ()))
TN_DIMS = (((0,), (0,)), ((), ()))

_CAND_ROWS = 80
_ID_SENTINEL = float(1 << 20)


def _cparams(sem, vmem_mb=None):
    kw = dict(dimension_semantics=sem)
    if vmem_mb is not None:
        kw["vmem_limit_bytes"] = vmem_mb * 2 ** 20
    return pltpu.CompilerParams(**kw)


def _split_dot(x, m_bf16, terms, left=False):
    acc = None
    r = x
    for _ in range(terms):
        p = r.astype(BF16)
        r = r - p.astype(F32)
        if left:
            y = jnp.dot(m_bf16, p, preferred_element_type=F32)
        else:
            y = jnp.dot(p, m_bf16, preferred_element_type=F32)
        acc = y if acc is None else acc + y
    return acc


def _sigmoid(x):
    return 1.0 / (1.0 + jnp.exp(-x))


def _norm_matmul_kernel(x_ref, g_ref, w_ref, o_ref, hn_ref):
    @pl.when(pl.program_id(1) == 0)
    def _():
        x = x_ref[...]
        ms = jnp.mean(x * x, axis=-1, keepdims=True)
        hn_ref[...] = ((x * lax.rsqrt(ms + EPS)) * g_ref[...]).astype(BF16)

    o_ref[...] = jnp.dot(hn_ref[...], w_ref[...], preferred_element_type=F32)


def _norm_matmul(x, g, w, *, tn, tc):
    n, d = x.shape
    c = w.shape[1]
    return pl.pallas_call(
        _norm_matmul_kernel,
        grid=(n // tn, c // tc),
        in_specs=[
            pl.BlockSpec((tn, d), lambda i, j: (i, 0)),
            pl.BlockSpec((1, d), lambda i, j: (0, 0)),
            pl.BlockSpec((d, tc), lambda i, j: (0, j)),
        ],
        out_specs=pl.BlockSpec((tn, tc), lambda i, j: (i, j)),
        out_shape=jax.ShapeDtypeStruct((n, c), F32),
        scratch_shapes=[pltpu.VMEM((tn, d), BF16)],
        compiler_params=_cparams(("parallel", "arbitrary"), vmem_mb=48),
        name="norm_matmul",
    )(x, g.reshape(1, d), w)


def _group_mean_matrix(width, group):
    r = np.arange(width)
    return jnp.asarray((r[:, None] // group == r[None, :] // group) / group, BF16)


def _prep_kernel(dq_ref, dk_ref, dv_ref, mq_ref, gq_ref, gk_ref, gm_ref, bd64_ref, bd128_ref,
                 qn_ref, kn_ref, knb_ref, vb_ref, qm_ref):
    def gnorm(x, bd, g):
        ms = _split_dot(x * x, bd, 3)
        return (x * lax.rsqrt(ms + EPS)) * g

    q = gnorm(dq_ref[...], bd64_ref[...], gq_ref[...])
    qn_ref[...] = (q * (DA_DK ** -0.5 * LOG2_E)).astype(BF16)
    k = gnorm(dk_ref[...], bd64_ref[...], gk_ref[...])
    kn_ref[...] = k
    knb_ref[...] = k.astype(BF16)
    vb_ref[...] = dv_ref[...].astype(BF16)
    qm_ref[...] = gnorm(mq_ref[...], bd128_ref[...], gm_ref[...]).astype(BF16)


def _prep(proj, gq, gk, gm, *, tn):
    n = proj.shape[0]
    w = BRANCH_W
    col = lambda c: pl.BlockSpec((tn, w), lambda i, c=c: (i, c))
    const = lambda shape: pl.BlockSpec(shape, lambda i: (0, 0))
    row_out = pl.BlockSpec((tn, w), lambda i: (i, 0))
    tile = lambda g: jnp.tile(g.astype(F32), w // g.shape[0]).reshape(1, w)
    return pl.pallas_call(
        _prep_kernel,
        grid=(n // tn,),
        in_specs=[col(4), col(5), col(6), col(7), const((1, w)), const((1, w)), const((1, w)),
                  const((w, w)), const((w, w))],
        out_specs=[row_out] * 5,
        out_shape=[jax.ShapeDtypeStruct((n, w), BF16), jax.ShapeDtypeStruct((n, w), F32),
                   jax.ShapeDtypeStruct((n, w), BF16), jax.ShapeDtypeStruct((n, w), BF16),
                   jax.ShapeDtypeStruct((n, w), BF16)],
        compiler_params=_cparams(("parallel",)),
        name="qk_norm",
    )(proj, proj, proj, proj, tile(gq), tile(gk), tile(gm),
      _group_mean_matrix(w, DA_DK), _group_mean_matrix(w, HEAD_W))


def _hgrn_kernel(*refs, layer, nb, tc, c, tvalid, has_s0):
    if has_s0:
        hq_ref, hf_ref, hi_ref, hg_ref, lbl_ref, og_ref, tri_ref, s0_ref = refs[:8]
        rest = refs[8:]
    else:
        hq_ref, hf_ref, hi_ref, hg_ref, lbl_ref, og_ref, tri_ref = refs[:7]
        s0_ref = None
        rest = refs[7:]
    o_ref, sout_ref, st_ref, q_s, k_s, b_s = rest
    t = pl.program_id(1)
    rows = nb * tc

    @pl.when(t == 0)
    def _():
        if has_s0:
            for ib in range(nb):
                for h in range(HEADS):
                    st_ref[ib, h] = s0_ref[ib, h].T
        else:
            st_ref[...] = jnp.zeros_like(st_ref)

    lg = lbl_ref[...]
    e = jnp.exp(lg - jnp.max(lg, axis=0, keepdims=True))
    pr = e / jnp.sum(e, axis=0, keepdims=True)
    lb = jnp.zeros((1, BRANCH_W), F32)
    for i in range(1, layer + 1):
        lb = lb + pr[i:i + 1]
    lb = jnp.clip(lb, 0.0, 1.0 - 1e-4)

    z = hf_ref[...]
    sg = _sigmoid(z)
    logf = jnp.log(lb + (1.0 - lb) * sg)
    kk = (1.0 - lb) * _sigmoid(-z)
    if tvalid < c:
        valid = (lax.broadcasted_iota(I32, (rows, BRANCH_W), 0) % c) < tvalid
        logf = jnp.where(valid, logf, 0.0)
        kk = jnp.where(valid, kk, 0.0)
    hq = hq_ref[...]
    q_s[...] = hq * _sigmoid(hq)
    k_s[...] = kk
    b_s[...] = _split_dot(logf, tri_ref[...], 3, left=True)

    og = og_ref[...]
    row_id = lax.broadcasted_iota(I32, (c, HEAD_W), 0)
    nsub = tc // c

    def sub(i, carry):
        r0 = pl.multiple_of(i * c, c)
        ib = i // nsub
        first = (i % nsub) == 0
        new_carry = []
        for h in range(HEADS):
            sl = slice(h * HEAD_W, (h + 1) * HEAD_W)
            bg = b_s[pl.ds(r0, c), sl]
            new_carry.append(bg[c - 1:c, :])
            b = bg - jnp.where(first, 0.0, carry[h])
            q = q_s[pl.ds(r0, c), sl]
            kx = k_s[pl.ds(r0, c), sl]
            iv = hi_ref[pl.ds(r0, c), sl]
            st = st_ref[ib, h]
            o_inter = lax.dot_general((q * jnp.exp(b)).astype(BF16), st.astype(BF16), NT_DIMS,
                                      preferred_element_type=F32)
            o_rows = []
            for tt in range(c):
                d = b[tt:tt + 1, :] - b
                w = jnp.exp(jnp.where(row_id <= tt, d, MASK_VALUE))
                p = (q[tt:tt + 1, :] * kx) * w
                a = jnp.sum(p, axis=-1, keepdims=True)
                o_rows.append(jnp.sum(a * iv, axis=0, keepdims=True))
            o = o_inter + jnp.concatenate(o_rows, axis=0)
            bl = b[c - 1:c, :]
            kd = kx * jnp.exp(bl - b)
            upd = lax.dot_general(iv.astype(BF16), kd.astype(BF16), TN_DIMS, preferred_element_type=F32)
            st_ref[ib, h] = jnp.exp(bl) * st + upd
            ms = jnp.mean(o * o, axis=-1, keepdims=True)
            hg = hg_ref[pl.ds(r0, c), sl]
            o_ref[pl.ds(r0, c), sl] = (((o * lax.rsqrt(ms + EPS)) * og) * (hg * _sigmoid(hg))).astype(o_ref.dtype)
        return tuple(new_carry)

    lax.fori_loop(0, nb * nsub, sub, tuple(jnp.zeros((1, HEAD_W), F32) for _ in range(HEADS)))

    @pl.when(t == pl.num_programs(1) - 1)
    def _():
        for ib in range(nb):
            for h in range(HEADS):
                sout_ref[ib, h] = st_ref[ib, h].T


def _hgrn(src, lb_logits, onorm_g, s0_all, *, layer, n_seq, seq_len, rows_out, row0, nb, tc, c, tvalid,
          out_dtype):
    rows = nb * tc
    nt = seq_len // tc
    rb0 = row0 // rows
    w = BRANCH_W
    col = lambda cc: pl.BlockSpec((rows, w), lambda b, t, cc=cc: (rb0 + b * nt + t, cc))
    r = np.arange(rows)
    tri = jnp.asarray((r[:, None] >= r[None, :]) & (r[:, None] // tc == r[None, :] // tc), BF16)
    in_specs = [col(0), col(1), col(2), col(3),
                pl.BlockSpec((DEPTH, w), lambda b, t: (0, 0)),
                pl.BlockSpec((1, HEAD_W), lambda b, t: (0, 0)),
                pl.BlockSpec((rows, rows), lambda b, t: (0, 0))]
    args = [src, src, src, src, lb_logits, onorm_g.reshape(1, HEAD_W), tri]
    has_s0 = s0_all is not None
    if has_s0:
        in_specs.append(pl.BlockSpec((None, nb, HEADS, HEAD_W, HEAD_W), lambda b, t: (layer, b, 0, 0, 0)))
        args.append(s0_all)
    kern = functools.partial(_hgrn_kernel, layer=layer, nb=nb, tc=tc, c=c, tvalid=tvalid, has_s0=has_s0)
    return pl.pallas_call(
        kern,
        grid=(n_seq // nb, nt),
        in_specs=in_specs,
        out_specs=[pl.BlockSpec((rows, w), lambda b, t: (b * nt + t, 0)),
                   pl.BlockSpec((nb, HEADS, HEAD_W, HEAD_W), lambda b, t: (b, 0, 0, 0))],
        out_shape=[jax.ShapeDtypeStruct((rows_out, w), out_dtype),
                   jax.ShapeDtypeStruct((n_seq, HEADS, HEAD_W, HEAD_W), F32)],
        scratch_shapes=[pltpu.VMEM((nb, HEADS, HEAD_W, HEAD_W), F32),
                        pltpu.VMEM((rows, w), F32), pltpu.VMEM((rows, w), F32), pltpu.VMEM((rows, w), F32)],
        compiler_params=_cparams(("parallel", "arbitrary")),
        name="hgrn2",
    )(*args)


def _lambda_value(lp, lam_init):
    a = jnp.sum(lp[0:1] * lp[1:2], axis=-1, keepdims=True)
    b = jnp.sum(lp[2:3] * lp[3:4], axis=-1, keepdims=True)
    return jnp.exp(a) - jnp.exp(b) + lam_init


def _dattn_prompt_kernel(iq_tab, ik_tab, lam_ref, q_ref, k_ref, v_ref, sg_ref, o_ref, q2_s, m_s, acc_s,
                         *, tq, rc, lam_init):
    step = pl.program_id(2)
    iq = iq_tab[step]
    ik = ik_tab[step]

    @pl.when(ik == 0)
    def _():
        q = q_ref[...].astype(F32)
        lane = lax.broadcasted_iota(I32, q.shape, 1)
        q2_s[0:tq, :] = jnp.where(lane < DA_DK, q, 0.0).astype(BF16)
        q2_s[tq:2 * tq, :] = jnp.where(lane >= DA_DK, q, 0.0).astype(BF16)
        m_s[...] = jnp.full_like(m_s, NEG_INF)
        acc_s[...] = jnp.zeros_like(acc_s)

    def update(masked):
        v1 = jnp.concatenate([v_ref[...], jnp.ones((tq, LANES), BF16)], axis=1)
        for c0 in range(0, 2 * tq, rc):
            rows = slice(c0, c0 + rc)
            nk = (c0 % tq) + rc if masked else tq
            s = lax.dot_general(q2_s[rows, :], k_ref[0:nk, :], NT_DIMS, preferred_element_type=F32)
            if masked:
                r = lax.broadcasted_iota(I32, s.shape, 0) + (c0 % tq)
                cidx = lax.broadcasted_iota(I32, s.shape, 1)
                s = jnp.where(cidx <= r, s, MASK_VALUE)
            m_prev = m_s[rows, :]
            m_new = jnp.maximum(m_prev, jnp.max(s, axis=-1, keepdims=True))
            alpha = jnp.exp2(m_prev - m_new)
            p = jnp.exp2(s - jnp.tile(m_new, (1, nk // LANES)))
            acc_s[rows, :] = (jnp.tile(alpha, (1, 2)) * acc_s[rows, :]
                              + jnp.dot(p.astype(BF16), v1[0:nk, :], preferred_element_type=F32))
            m_s[rows, :] = m_new

    @pl.when(ik < iq)
    def _():
        update(False)

    @pl.when(ik == iq)
    def _():
        update(True)
        lam = _lambda_value(lam_ref[...], lam_init)
        o0 = acc_s[0:tq, 0:HEAD_W] / acc_s[0:tq, HEAD_W:]
        o1 = acc_s[tq:2 * tq, 0:HEAD_W] / acc_s[tq:2 * tq, HEAD_W:]
        da = o0 - lam * o1
        ms = jnp.mean(da * da, axis=-1, keepdims=True)
        o_ref[...] = (((da * lax.rsqrt(ms + EPS)) * sg_ref[...]) * (1.0 - lam_init)).astype(o_ref.dtype)


def _dattn_prompt(qn, knb, vb, lam_p, subln_g, *, n_seq, seq_len, rows_out, tq, rc, lam_init):
    nq = seq_len // tq
    pairs = [(iq, ik) for iq in range(nq) for ik in range(iq + 1)]
    iq_tab = jnp.asarray([p[0] for p in pairs], I32)
    ik_tab = jnp.asarray([p[1] for p in pairs], I32)
    qspec = pl.BlockSpec((tq, HEAD_W), lambda b, h, s, iqt, ikt: (b * nq + iqt[s], h))
    kspec = pl.BlockSpec((tq, HEAD_W), lambda b, h, s, iqt, ikt: (b * nq + ikt[s], h))
    grid_spec = pltpu.PrefetchScalarGridSpec(
        num_scalar_prefetch=2,
        grid=(n_seq, HEADS, len(pairs)),
        in_specs=[pl.BlockSpec((4, DA_DK), lambda b, h, s, iqt, ikt: (0, 0)), qspec, kspec, kspec,
                  pl.BlockSpec((1, HEAD_W), lambda b, h, s, iqt, ikt: (0, 0))],
        out_specs=qspec,
        scratch_shapes=[pltpu.VMEM((2 * tq, HEAD_W), BF16), pltpu.VMEM((2 * tq, LANES), F32),
                        pltpu.VMEM((2 * tq, HEAD_W + LANES), F32)],
    )
    return pl.pallas_call(
        functools.partial(_dattn_prompt_kernel, tq=tq, rc=rc, lam_init=lam_init),
        grid_spec=grid_spec,
        out_shape=jax.ShapeDtypeStruct((rows_out, BRANCH_W), BF16),
        compiler_params=_cparams(("parallel", "parallel", "arbitrary")),
        name="diff_attn_prompt",
    )(iq_tab, ik_tab, lam_p, qn, knb, vb, subln_g.reshape(1, HEAD_W))


def _dattn_sample_kernel(pt_ref, lam_ref, qr_ref, kc_ref, vc_ref, sg_ref, *rest, n_pages, t_new, lam_init):
    k_refs = rest[:n_pages]
    v_refs = rest[n_pages:2 * n_pages]
    o_ref = rest[2 * n_pages]
    nrow = 2 * HEADS * t_new
    qr = qr_ref[...]
    s_past = jnp.concatenate(
        [jnp.dot(qr, k[...].astype(BF16), preferred_element_type=F32) for k in k_refs], axis=1)

    qf = qr.astype(F32)
    kc = kc_ref[...]
    vc = vc_ref[...]
    t_of_row = lax.broadcasted_iota(I32, (nrow, 1), 0) % t_new
    s_new = []
    for j in range(t_new):
        sj = jnp.sum(qf * kc[j:j + 1, :], axis=-1, keepdims=True)
        s_new.append(jnp.where(t_of_row >= j, sj, MASK_VALUE))
    m = jnp.max(s_past, axis=-1, keepdims=True)
    for sj in s_new:
        m = jnp.maximum(m, sj)
    p_past = jnp.exp2(s_past - m)
    p_new = [jnp.exp2(sj - m) for sj in s_new]
    l = jnp.sum(p_past, axis=-1, keepdims=True)
    for pj in p_new:
        l = l + pj

    lam = _lambda_value(lam_ref[...], lam_init)
    rows_h = 2 * t_new
    for h in range(HEADS):
        sl = slice(h * HEAD_W, (h + 1) * HEAD_W)
        r0 = h * rows_h
        v_h = jnp.concatenate([v[pl.ds(h, PAGE_SIZE, stride=HEADS), :] for v in v_refs], axis=0)
        acc = jnp.dot(p_past[r0:r0 + rows_h].astype(BF16), v_h.astype(BF16), preferred_element_type=F32)
        for j in range(t_new):
            acc = acc + p_new[j][r0:r0 + rows_h] * vc[j:j + 1, sl]
        on = acc / l[r0:r0 + rows_h]
        da = on[:t_new] - lam * on[t_new:]
        ms = jnp.mean(da * da, axis=-1, keepdims=True)
        o_ref[:, sl] = (((da * lax.rsqrt(ms + EPS)) * sg_ref[...]) * (1.0 - lam_init)).astype(o_ref.dtype)


def _dattn_sample(page_table, qrows, kcur, vcur, cache_kt, cache_v, lam_p, subln_g, *, layer, lam_init):
    n_seq, n_pages = page_table.shape
    t_new = kcur.shape[1]
    nrow = qrows.shape[1]
    per_seq = lambda shape: pl.BlockSpec((None,) + shape, lambda b, pt: (b, 0, 0))
    page = lambda p: pl.BlockSpec((None, None, PAGE_SIZE * HEADS, HEAD_W),
                                  lambda b, pt, p=p: (layer, pt[b * n_pages + p], 0, 0))
    grid_spec = pltpu.PrefetchScalarGridSpec(
        num_scalar_prefetch=1,
        grid=(n_seq,),
        in_specs=[pl.BlockSpec((4, DA_DK), lambda b, pt: (0, 0)),
                  per_seq((nrow, BRANCH_W)), per_seq((t_new, BRANCH_W)), per_seq((t_new, BRANCH_W)),
                  pl.BlockSpec((1, HEAD_W), lambda b, pt: (0, 0))]
                 + [page(p) for p in range(n_pages)] * 2,
        out_specs=per_seq((t_new, BRANCH_W)),
    )
    return pl.pallas_call(
        functools.partial(_dattn_sample_kernel, n_pages=n_pages, t_new=t_new, lam_init=lam_init),
        grid_spec=grid_spec,
        out_shape=jax.ShapeDtypeStruct((n_seq, t_new, BRANCH_W), F32),
        compiler_params=_cparams(("parallel",), vmem_mb=40),
        name="diff_attn_sample",
    )(page_table.reshape(-1), lam_p, qrows, kcur, vcur, subln_g.reshape(1, HEAD_W),
      *([cache_kt] * n_pages), *([cache_v] * n_pages))


def _mem_kv_kernel(m_ref, g_ref, w_ref, kg_ref, bd_ref, mk_ref, mv_ref):
    x = m_ref[...]
    ms = jnp.mean(x * x, axis=-1, keepdims=True)
    hn = ((x * lax.rsqrt(ms + EPS)) * g_ref[...]).astype(BF16)
    kv = jnp.dot(hn, w_ref[...], preferred_element_type=F32)
    k = kv[:, :BRANCH_W]
    ms = _split_dot(k * k, bd_ref[...], 3)
    mk_ref[...] = (k * lax.rsqrt(ms + EPS)) * kg_ref[...]
    mv_ref[...] = kv[:, BRANCH_W:]


def _mem_kv(mem, norm_g, w_kv, knorm_g):
    n_seq = mem.shape[0]
    w = BRANCH_W
    out = pl.BlockSpec((None, None, MEM_LEN, w), lambda l, b: (l, b, 0, 0))
    return pl.pallas_call(
        _mem_kv_kernel,
        grid=(DEPTH, n_seq),
        in_specs=[pl.BlockSpec((None, MEM_LEN, D_MODEL), lambda l, b: (b, 0, 0)),
                  pl.BlockSpec((None, 1, D_MODEL), lambda l, b: (l, 0, 0)),
                  pl.BlockSpec((None, D_MODEL, 2 * w), lambda l, b: (l, 0, 0)),
                  pl.BlockSpec((None, 1, w), lambda l, b: (l, 0, 0)),
                  pl.BlockSpec((w, w), lambda l, b: (0, 0))],
        out_specs=[out, out],
        out_shape=[jax.ShapeDtypeStruct((DEPTH, n_seq, MEM_LEN, w), F32)] * 2,
        compiler_params=_cparams(("parallel", "parallel")),
        name="mem_kv",
    )(mem, norm_g.reshape(DEPTH, 1, D_MODEL), w_kv,
      jnp.tile(knorm_g.astype(F32), (1, HEADS)).reshape(DEPTH, 1, w), _group_mean_matrix(w, HEAD_W))


def _cross_kernel(q_ref, mk_ref, mv_ref, o_ref, *, head_rows, n_seq_blk):
    tq = q_ref.shape[0] // n_seq_blk
    for r in range(n_seq_blk):
        q = q_ref[r * tq:(r + 1) * tq, :]
        for h in range(HEADS):
            sl = slice(h * HEAD_W, (h + 1) * HEAD_W)
            if head_rows:
                m0 = r * MEM_LEN * HEADS + h
                mk = mk_ref[pl.ds(m0, MEM_LEN, stride=HEADS), :].astype(BF16)
                mv = mv_ref[pl.ds(m0, MEM_LEN, stride=HEADS), :].astype(BF16)
            else:
                mk = mk_ref[r * MEM_LEN:(r + 1) * MEM_LEN, sl].astype(BF16)
                mv = mv_ref[r * MEM_LEN:(r + 1) * MEM_LEN, sl].astype(BF16)
            s = lax.dot_general(q[:, sl], mk, NT_DIMS, preferred_element_type=F32) * (HEAD_W ** -0.5)
            p = jnp.exp(s - jnp.max(s, axis=-1, keepdims=True))
            l = jnp.sum(p, axis=-1, keepdims=True)
            o = jnp.dot(p.astype(BF16), mv, preferred_element_type=F32) / l
            o_ref[r * tq:(r + 1) * tq, sl] = o.astype(o_ref.dtype)


def _cross_prompt(qm, mk_all, mv_all, *, layer, n_seq, seq_len, rows_out, tq):
    nq = seq_len // tq
    mem = pl.BlockSpec((None, None, MEM_LEN, BRANCH_W), lambda b, i: (layer, b, 0, 0))
    row = pl.BlockSpec((tq, BRANCH_W), lambda b, i: (b * nq + i, 0))
    return pl.pallas_call(
        functools.partial(_cross_kernel, head_rows=False, n_seq_blk=1),
        grid=(n_seq, nq),
        in_specs=[row, mem, mem],
        out_specs=row,
        out_shape=jax.ShapeDtypeStruct((rows_out, BRANCH_W), BF16),
        compiler_params=_cparams(("parallel", "parallel")),
        name="cross_attn_prompt",
    )(qm, mk_all, mv_all)


def _cross_sample(qm_pad, mk_all, mv_all, *, layer):
    n_seq, tpad, w = qm_pad.shape
    nb = CROSS_SAMPLE_SEQS
    depth = mk_all.shape[0]
    mem_rows = nb * MEM_LEN * HEADS
    mem = pl.BlockSpec((None, None, mem_rows, HEAD_W), lambda b: (layer, b, 0, 0))
    row = pl.BlockSpec((None, nb * tpad, w), lambda b: (b, 0, 0))
    grouped = lambda m: m.reshape(depth, n_seq // nb, mem_rows, HEAD_W)
    return pl.pallas_call(
        functools.partial(_cross_kernel, head_rows=True, n_seq_blk=nb),
        grid=(n_seq // nb,),
        in_specs=[row, mem, mem],
        out_specs=row,
        out_shape=jax.ShapeDtypeStruct((n_seq // nb, nb * tpad, w), F32),
        compiler_params=_cparams(("parallel",)),
        name="cross_attn_sample",
    )(qm_pad.reshape(n_seq // nb, nb * tpad, w), grouped(mk_all), grouped(mv_all)).reshape(n_seq, tpad, w)


def _merge_kernel(x_ref, g0_ref, g1_ref, g2_ref, a0_ref, a1_ref, a2_ref, w0_ref, w1_ref, w2_ref, wo_ref, o_ref):
    merged = None
    for g_ref, a_ref, w_ref in ((g0_ref, a0_ref, w0_ref), (g1_ref, a1_ref, w1_ref), (g2_ref, a2_ref, w2_ref)):
        y = _sigmoid(g_ref[...]) * jnp.dot(a_ref[...], w_ref[...], preferred_element_type=F32)
        merged = y if merged is None else merged + y
    o_ref[...] = x_ref[...] + jnp.dot(merged.astype(BF16), wo_ref[...], preferred_element_type=F32)


def _merge(x, proj, hg, da, cm, w_hg, w_da, w_mem, w_out, *, tn):
    n, d = x.shape
    w = BRANCH_W
    gate_col0 = (8 * w) // d
    row = lambda width: pl.BlockSpec((tn, width), lambda i: (i, 0))
    gate = lambda c: pl.BlockSpec((tn, d), lambda i, c=c: (i, gate_col0 + c))
    const = lambda shape: pl.BlockSpec(shape, lambda i: (0, 0))
    return pl.pallas_call(
        _merge_kernel,
        grid=(n // tn,),
        in_specs=[row(d), gate(0), gate(1), gate(2), row(w), row(w), row(w),
                  const((w, d)), const((w, d)), const((w, d)), const((d, d))],
        out_specs=row(d),
        out_shape=jax.ShapeDtypeStruct((n, d), F32),
        compiler_params=_cparams(("parallel",), vmem_mb=48),
        name="branch_merge",
    )(x, proj, proj, proj, hg, da, cm, w_hg, w_da, w_mem, w_out)


def _topk_rows(scores, k, key_id):
    scores = list(scores)
    vals = [[] for _ in scores]
    ids = [[] for _ in scores]
    for _ in range(k):
        for n, s in enumerate(scores):
            m = jnp.max(s, axis=0, keepdims=True)
            sel = jnp.min(jnp.where(s == m, key_id, _ID_SENTINEL), axis=0, keepdims=True)
            scores[n] = jnp.where(key_id == sel, NEG_INF, s)
            vals[n].append(m)
            ids[n].append(sel)
        yield None
    return vals, ids


def _candidate_tables(tt):
    flat = np.full((_CAND_ROWS,), _ID_SENTINEL, np.float64)
    flat[0:16] = np.arange(16)
    flat[16:24] = 16 + np.arange(8)
    for j1 in range(2, 8):
        lim = PK_TOPK // (j1 + 1)
        flat[24 + (j1 - 2) * 8:24 + (j1 - 2) * 8 + lim] = j1 * 16 + np.arange(lim)
    flat[72:80] = (8 + np.arange(8)) * 16
    return jnp.asarray(np.broadcast_to(flat[:, None], (_CAND_ROWS, tt)), F32)


def _norm_bf16(x, g):
    ms = jnp.mean(x * x, axis=-1, keepdims=True)
    return ((x * lax.rsqrt(ms + EPS)) * g).astype(BF16)


def _select_head_steps(h, hn, wq_ref, sk_ref, cflat, eid_t, gate_t, lane0=0):
    tt = hn.shape[0]
    cflat = cflat[:, :tt]
    key_id = lax.broadcasted_iota(I32, (N_KEYS, tt), 0).astype(F32)
    cvalid = cflat < float(PK_TOPK * PK_TOPK)
    qh = jnp.dot(hn, wq_ref[h], preferred_element_type=F32).astype(BF16)
    sc = lax.dot_general(sk_ref[h], qh, NT_DIMS, preferred_element_type=F32)
    yield None
    (v1, v2), (i1, i2) = yield from _topk_rows([sc[:N_KEYS], sc[N_KEYS:]], PK_TOPK, key_id)
    v2a = jnp.concatenate(v2, axis=0)
    i2a = jnp.concatenate(i2, axis=0)
    pieces_v = [v1[0] + v2a, v1[1] + v2a[:8]]
    pieces_e = [i1[0] * N_KEYS + i2a, i1[1] * N_KEYS + i2a[:8]]
    for j1 in range(2, 8):
        pieces_v.append(v1[j1] + v2a[:8])
        pieces_e.append(i1[j1] * N_KEYS + i2a[:8])
    pieces_v.append(jnp.concatenate(v1[8:], axis=0) + v2a[:1])
    pieces_e.append(jnp.concatenate(i1[8:], axis=0) * N_KEYS + i2a[:1])
    cand = jnp.where(cvalid, jnp.concatenate(pieces_v, axis=0), NEG_INF)
    ce = jnp.concatenate(pieces_e, axis=0)
    fv, fe = [], []
    for _ in range(PK_TOPK):
        m = jnp.max(cand, axis=0, keepdims=True)
        sel = jnp.min(jnp.where(cand == m, cflat, _ID_SENTINEL), axis=0, keepdims=True)
        hit = cflat == sel
        fe.append(jnp.max(jnp.where(hit, ce, -1.0), axis=0, keepdims=True))
        cand = jnp.where(hit, NEG_INF, cand)
        fv.append(m)
        yield None
    fva = jnp.concatenate(fv, axis=0)
    ex = jnp.exp(fva - fva[0:1])
    r0 = pl.multiple_of(h * PK_TOPK, PK_TOPK)
    gate_t[pl.ds(r0, PK_TOPK), lane0:lane0 + tt] = ex / jnp.sum(ex, axis=0, keepdims=True)
    eid_t[pl.ds(r0, PK_TOPK), lane0:lane0 + tt] = jnp.concatenate(fe, axis=0)


def _select_head(*args):
    for _ in _select_head_steps(*args):
        pass


def _build_routing(eid_s, gate_s, w_s, *, tn, pitch):
    n_sel = PK_HEADS * PK_TOPK
    row_id = lax.broadcasted_iota(I32, (N_KEYS, n_sel), 0).astype(F32).astype(BF16)
    one = jnp.ones((N_KEYS, n_sel), BF16)
    zero = jnp.zeros((N_KEYS, n_sel), BF16)
    bf16_rows = 16

    def all_rows(v):
        return jnp.tile(jnp.broadcast_to(v, (bf16_rows, n_sel)).astype(BF16), (N_KEYS // bf16_rows, 1))

    group = 64

    def build(n8, carry):
        n0 = pl.multiple_of(n8 * group, group)
        e8 = eid_s[pl.ds(n0, group), :]
        g8 = gate_s[pl.ds(n0, group), :]
        for r in range(group):
            e = e8[r:r + 1, :]
            g = 0.5 * g8[r:r + 1, :]
            a_id = jnp.floor(e * (1.0 / N_KEYS))
            sel_a = jnp.where(row_id == all_rows(a_id), one, zero)
            hit_b = row_id == all_rows(e - a_id * N_KEYS)
            g_hi = g.astype(BF16).astype(F32)
            g_lo = g - g_hi
            lhs = jnp.concatenate([sel_a, sel_a], axis=1)
            rhs = jnp.concatenate([jnp.where(hit_b, all_rows(g_hi), zero),
                                   jnp.where(hit_b, all_rows(g_lo), zero)], axis=1)
            w = lax.dot_general(lhs, rhs, NT_DIMS, preferred_element_type=F32)
            w_s[pl.ds(n0 + r, N_KEYS, stride=pitch), :] = w
        return carry

    lax.fori_loop(0, tn // group, build, 0)


def _peer_kernel(xc_ref, xn_ref, g_ref, wq_ref, sk_ref, cflat_ref, u_ref, v_ref, o_ref,
                 w_s, acc_s, hn_s, eid_t, gate_t, eid_s, gate_s, *, tn, te, pitch):
    i = pl.program_id(0)
    j = pl.program_id(1)
    cur = i % 2
    nxt = 1 - cur
    cflat = cflat_ref[...]

    @pl.when(j == 0)
    def _():
        @pl.when(i == 0)
        def _():
            hn0 = _norm_bf16(xc_ref[...], g_ref[...])
            hn_s[0] = hn0

            def head(h, carry):
                _select_head(h, hn0, wq_ref, sk_ref, cflat, eid_t, gate_t)
                return carry

            lax.fori_loop(0, PK_HEADS, head, 0)
            eid_s[...] = eid_t[...].T
            gate_s[...] = gate_t[...].T

        _build_routing(eid_s, gate_s, w_s, tn=tn, pitch=pitch)
        acc_s[...] = jnp.zeros_like(acc_s)
        hn_s[nxt] = _norm_bf16(xn_ref[...], g_ref[...])

    sub = LANES
    hn_next = hn_s[nxt]
    selects = [_select_head_steps(j, hn_next[t0:t0 + sub], wq_ref, sk_ref, cflat, eid_t, gate_t, lane0=t0)
               for t0 in range(0, tn, sub)]

    def advance(units):
        for _ in range(units):
            while selects and next(selects[0], "done") == "done":
                selects.pop(0)
                if not selects:
                    return

    advance(1)
    units_per_piece = 8
    hn = hn_s[cur]
    acc = acc_s[...]
    e_chunk = 2 * LANES
    v_chunk = 8 * e_chunk
    for c0 in range(0, te, v_chunk):
        pieces = []
        for c1 in range(c0, c0 + v_chunk, e_chunk):
            a = lax.dot_general(hn, u_ref[c1:c1 + e_chunk, :], NT_DIMS, preferred_element_type=F32)
            advance(units_per_piece // 2)
            for r in range(e_chunk // LANES):
                ar = a[:, r * LANES:(r + 1) * LANES]
                a_id = j * (te // LANES) + c1 // LANES + r
                wr = w_s[pl.ds(pl.multiple_of(a_id * pitch, 8), tn), :]
                act = ar * (1.0 + lax.erf(ar * (0.5 ** 0.5)))
                pieces.append((wr * act).astype(BF16))
            advance(units_per_piece - units_per_piece // 2)
        acc = acc + jnp.dot(jnp.concatenate(pieces, axis=1), v_ref[c0:c0 + v_chunk, :],
                            preferred_element_type=F32)
        advance(units_per_piece)
    acc_s[...] = acc
    advance(len(selects) * 64)

    @pl.when(j == pl.num_programs(1) - 1)
    def _():
        o_ref[...] = xc_ref[...] + acc_s[...]
        eid_s[...] = eid_t[...].T
        gate_s[...] = gate_t[...].T


def _peer(x, g, wq_heads, sk_pad, u_all, v_all, *, layer, tn, te):
    n, d = x.shape
    nt = n // tn
    assert N_EXPERTS // te == PK_HEADS
    sel = PK_HEADS * PK_TOPK
    pitch = tn + 8
    table = pl.BlockSpec((None, te, d), lambda i, j: (layer, j, 0))
    const = lambda shape: pl.BlockSpec(shape, lambda i, j: (0,) * len(shape))
    return pl.pallas_call(
        functools.partial(_peer_kernel, tn=tn, te=te, pitch=pitch),
        grid=(nt, N_EXPERTS // te),
        in_specs=[pl.BlockSpec((tn, d), lambda i, j: (i, 0)),
                  pl.BlockSpec((tn, d), lambda i, j: (jnp.minimum(i + 1, nt - 1), 0)),
                  const((1, d)), const((PK_HEADS, d, LANES)), const((PK_HEADS, 2 * N_KEYS, LANES)),
                  const((_CAND_ROWS, tn)), table, table],
        out_specs=pl.BlockSpec((tn, d), lambda i, j: (i, 0)),
        out_shape=jax.ShapeDtypeStruct((n, d), F32),
        scratch_shapes=[pltpu.VMEM((N_KEYS * pitch, LANES), F32), pltpu.VMEM((tn, d), F32),
                        pltpu.VMEM((2, tn, d), BF16),
                        pltpu.VMEM((sel, tn), F32), pltpu.VMEM((sel, tn), F32),
                        pltpu.VMEM((tn, sel), F32), pltpu.VMEM((tn, sel), F32)],
        compiler_params=_cparams(("arbitrary", "arbitrary"), vmem_mb=60),
        name="peer",
    )(x, x, g.reshape(1, d), wq_heads, sk_pad, _candidate_tables(tn), u_all, v_all)


def _pad_tokens(a, tpad):
    return jnp.pad(a, ((0, 0), (0, tpad - a.shape[1]), (0, 0)))


def kernel(x_prompt, x_sample, cache_diff_k, cache_diff_v, cache_mem_k, cache_mem_v, state_hgrn, page_table, mem_prompt, norm_mix_g, w_in, hg_lb_logits, hg_onorm_g, da_qnorm_g, da_knorm_g, da_lambda, da_subln_g, mem_norm_g, w_mem_kv, mem_qnorm_g, mem_knorm_g, w_branch_hg, w_branch_da, w_branch_mem, w_out, norm_ffn_g, peer_wq, peer_subkeys, peer_u, peer_v):
    n_p, seq, d = x_prompt.shape
    n_s, t_new, _ = x_sample.shape
    rows_p = n_p * seq
    rows_s = n_s * t_new
    w = BRANCH_W
    depth = w_in.shape[0]

    w_in_b = w_in.astype(BF16)
    w_kv_b = w_mem_kv.astype(BF16)
    w_hg_b, w_da_b, w_cm_b, w_out_b = (a.astype(BF16) for a in (w_branch_hg, w_branch_da, w_branch_mem, w_out))
    wq_heads = peer_wq.astype(BF16).reshape(depth, d, PK_HEADS, LANES).transpose(0, 2, 1, 3)
    sk = peer_subkeys.astype(BF16)
    zk = jnp.zeros_like(sk[:, :, 0])
    sk_pad = jnp.concatenate([jnp.concatenate([sk[:, :, 0], zk], axis=-1),
                              jnp.concatenate([zk, sk[:, :, 1]], axis=-1)], axis=2)
    u_b = peer_u.astype(BF16)
    v_b = peer_v.astype(BF16)
    n_pool = cache_diff_k.shape[1]
    cache_kt = jnp.transpose(cache_diff_k, (0, 1, 3, 4, 5, 2)).reshape(depth, n_pool, w, PAGE_SIZE)
    cache_v = cache_diff_v.reshape(depth, n_pool, PAGE_SIZE * HEADS, HEAD_W)
    cmem_k = cache_mem_k.reshape(depth, n_s, MEM_LEN * HEADS, HEAD_W)
    cmem_v = cache_mem_v.reshape(depth, n_s, MEM_LEN * HEADS, HEAD_W)
    lb_logits = hg_lb_logits.astype(F32)
    rows_all = rows_p + rows_s
    with_sample = lambda full, part: lax.dynamic_update_slice(full, part, (rows_p, 0))

    mk_p, mv_p = _mem_kv(mem_prompt, mem_norm_g, w_kv_b, mem_knorm_g)

    x = jnp.concatenate([x_prompt.reshape(rows_p, d), x_sample.reshape(rows_s, d)], axis=0)
    row_mask = (np.arange(2 * HEADS * t_new)[:, None] // t_new) == (np.arange(w)[None, :] // DA_DK)
    row_mask = jnp.asarray(row_mask)
    tpad_h = 8
    tpad_c = 16

    hg_p, hg_s, dk_p, dv_p, dk_s, dv_s = [], [], [], [], [], []
    for l in range(depth):
        lam_init = 0.8 - 0.6 * math.exp(-0.3 * l)
        lam_p = da_lambda[l].astype(F32)
        proj = _norm_matmul(x, norm_mix_g[l], w_in_b[l], tn=PROJ_ROW_TILE, tc=PROJ_COL_TILE)
        qn, kn, knb, vb, qm = _prep(proj, da_qnorm_g[l], da_knorm_g[l], mem_qnorm_g[l], tn=ROW_TILE)

        hgo_p, st_p = _hgrn(proj, lb_logits, hg_onorm_g[l], None, layer=l, n_seq=n_p, seq_len=seq,
                            rows_out=rows_all, row0=0, nb=1, tc=HGRN_BLOCK, c=HGRN_CHUNK, tvalid=HGRN_CHUNK,
                            out_dtype=BF16)
        src_s = _pad_tokens(proj[rows_p:, :4 * w].reshape(n_s, t_new, 4 * w), tpad_h).reshape(n_s * tpad_h, 4 * w)
        hgo_s, st_s = _hgrn(src_s, lb_logits, hg_onorm_g[l], state_hgrn, layer=l, n_seq=n_s, seq_len=tpad_h,
                            rows_out=n_s * tpad_h, row0=0, nb=8, tc=tpad_h, c=tpad_h, tvalid=t_new, out_dtype=F32)
        hgo_s = hgo_s.reshape(n_s, tpad_h, w)[:, :t_new].reshape(rows_s, w).astype(BF16)

        da_p = _dattn_prompt(qn, knb, vb, lam_p, da_subln_g[l], n_seq=n_p, seq_len=seq, rows_out=rows_all,
                             tq=ATTN_TILE, rc=ATTN_ROW_CHUNK, lam_init=lam_init)
        qs = qn[rows_p:].reshape(n_s, t_new, w)
        qrows = jnp.where(row_mask, jnp.tile(qs, (1, 2 * HEADS, 1)), jnp.zeros((), BF16))
        da_s = _dattn_sample(page_table, qrows, knb[rows_p:].reshape(n_s, t_new, w).astype(F32),
                             vb[rows_p:].reshape(n_s, t_new, w).astype(F32), cache_kt, cache_v, lam_p,
                             da_subln_g[l], layer=l, lam_init=lam_init)
        da_s = da_s.reshape(rows_s, w).astype(BF16)

        cm_p = _cross_prompt(qm, mk_p, mv_p, layer=l, n_seq=n_p, seq_len=seq, rows_out=rows_all, tq=ATTN_TILE)
        qm_s = _pad_tokens(qm[rows_p:].reshape(n_s, t_new, w), tpad_c)
        cm_s = _cross_sample(qm_s, cmem_k, cmem_v, layer=l)[:, :t_new].reshape(rows_s, w).astype(BF16)

        x = _merge(x, proj, with_sample(hgo_p, hgo_s), with_sample(da_p, da_s), with_sample(cm_p, cm_s),
                   w_hg_b[l], w_da_b[l], w_cm_b[l], w_out_b[l], tn=ROW_TILE)

        x = _peer(x, norm_ffn_g[l], wq_heads[l], sk_pad[l], u_b, v_b, layer=l,
                  tn=PEER_TOKEN_TILE, te=PEER_EXPERT_TILE)

        hg_p.append(st_p)
        hg_s.append(st_s)
        dk_p.append(kn[:rows_p].reshape(n_p, seq, HEADS, 2, DA_DK))
        dv_p.append(proj[:rows_p, 6 * w:7 * w].reshape(n_p, seq, HEADS, HEAD_W))
        dk_s.append(kn[rows_p:].reshape(n_s, t_new, HEADS, 2, DA_DK))
        dv_s.append(proj[rows_p:, 6 * w:7 * w].reshape(n_s, t_new, HEADS, HEAD_W))

    mem_shape = (depth, n_p, MEM_LEN, HEADS, HEAD_W)
    return (x[:rows_p].reshape(n_p, seq, d), x[rows_p:].reshape(n_s, t_new, d),
            jnp.stack(hg_p), jnp.stack(hg_s), jnp.stack(dk_p), jnp.stack(dv_p),
            jnp.stack(dk_s), jnp.stack(dv_s), mk_p.reshape(mem_shape), mv_p.reshape(mem_shape))
```

```python
import functools
import math

import numpy as np
import jax
import jax.numpy as jnp
from jax import lax
from jax.experimental import pallas as pl
from jax.experimental.pallas import tpu as pltpu

F32 = jnp.float32
BF16 = jnp.bfloat16
I32 = jnp.int32

D_MODEL = 1024
DEPTH = 4
PAGE_SIZE = 128
HEADS = 4
HEAD_W = 128
BRANCH_W = HEADS * HEAD_W
DA_DK = 64
MEM_LEN = 256
N_BRANCH = 3
IN_COLS = 8 * BRANCH_W + N_BRANCH * D_MODEL
PK_HEADS = 8
N_KEYS = 128
N_EXPERTS = N_KEYS * N_KEYS
PK_TOPK = 16
EPS = 1e-6
MASK_VALUE = -1e30
NEG_INF = float("-inf")
LOG2_E = math.log2(math.e)

LANES = 128

ROW_TILE = 512
PROJ_ROW_TILE = 1536
PROJ_COL_TILE = 1024
ATTN_TILE = 1024
ATTN_ROW_CHUNK = 512
HGRN_BLOCK = 512
HGRN_CHUNK = 16
CROSS_SAMPLE_SEQS = 4
PEER_TOKEN_TILE = 256
PEER_EXPERT_TILE = 2048
NT_DIMS = (((1,), (1,)), ((), ()))
TN_DIMS = (((0,), (0,)), ((), ()))

_CAND_ROWS = 80
_ID_SENTINEL = float(1 << 20)


def _cparams(sem, vmem_mb=None):
    kw = dict(dimension_semantics=sem)
    if vmem_mb is not None:
        kw["vmem_limit_bytes"] = vmem_mb * 2 ** 20
    return pltpu.CompilerParams(**kw)


def _split_dot(x, m_bf16, terms, left=False):
    acc = None
    r = x
    for _ in range(terms):
        p = r.astype(BF16)
        r = r - p.astype(F32)
        if left:
            y = jnp.dot(m_bf16, p, preferred_element_type=F32)
        else:
            y = jnp.dot(p, m_bf16, preferred_element_type=F32)
        acc = y if acc is None else acc + y
    return acc


def _sigmoid(x):
    return 1.0 / (1.0 + jnp.exp(-x))


def _norm_matmul_kernel(x_ref, g_ref, w_ref, o_ref, hn_ref):
    @pl.when(pl.program_id(1) == 0)
    def _():
        x = x_ref[...]
        ms = jnp.mean(x * x, axis=-1, keepdims=True)
        hn_ref[...] = ((x * lax.rsqrt(ms + EPS)) * g_ref[...]).astype(BF16)

    o_ref[...] = jnp.dot(hn_ref[...], w_ref[...], preferred_element_type=F32)


def _norm_matmul(x, g, w, *, tn, tc):
    n, d = x.shape
    c = w.shape[1]
    return pl.pallas_call(
        _norm_matmul_kernel,
        grid=(n // tn, c // tc),
        in_specs=[
            pl.BlockSpec((tn, d), lambda i, j: (i, 0)),
            pl.BlockSpec((1, d), lambda i, j: (0, 0)),
            pl.BlockSpec((d, tc), lambda i, j: (0, j)),
        ],
        out_specs=pl.BlockSpec((tn, tc), lambda i, j: (i, j)),
        out_shape=jax.ShapeDtypeStruct((n, c), F32),
        scratch_shapes=[pltpu.VMEM((tn, d), BF16)],
        compiler_params=_cparams(("parallel", "arbitrary"), vmem_mb=48),
        name="norm_matmul",
    )(x, g.reshape(1, d), w)


def _group_mean_matrix(width, group):
    r = np.arange(width)
    return jnp.asarray((r[:, None] // group == r[None, :] // group) / group, BF16)


def _prep_kernel(dq_ref, dk_ref, dv_ref, mq_ref, gq_ref, gk_ref, gm_ref, bd64_ref, bd128_ref,
                 qn_ref, kn_ref, knb_ref, vb_ref, qm_ref):
    def gnorm(x, bd, g):
        ms = _split_dot(x * x, bd, 3)
        return (x * lax.rsqrt(ms + EPS)) * g

    q = gnorm(dq_ref[...], bd64_ref[...], gq_ref[...])
    qn_ref[...] = (q * (DA_DK ** -0.5 * LOG2_E)).astype(BF16)
    k = gnorm(dk_ref[...], bd64_ref[...], gk_ref[...])
    kn_ref[...] = k
    knb_ref[...] = k.astype(BF16)
    vb_ref[...] = dv_ref[...].astype(BF16)
    qm_ref[...] = gnorm(mq_ref[...], bd128_ref[...], gm_ref[...]).astype(BF16)


def _prep(proj, gq, gk, gm, *, tn):
    n = proj.shape[0]
    w = BRANCH_W
    col = lambda c: pl.BlockSpec((tn, w), lambda i, c=c: (i, c))
    const = lambda shape: pl.BlockSpec(shape, lambda i: (0, 0))
    row_out = pl.BlockSpec((tn, w), lambda i: (i, 0))
    tile = lambda g: jnp.tile(g.astype(F32), w // g.shape[0]).reshape(1, w)
    return pl.pallas_call(
        _prep_kernel,
        grid=(n // tn,),
        in_specs=[col(4), col(5), col(6), col(7), const((1, w)), const((1, w)), const((1, w)),
                  const((w, w)), const((w, w))],
        out_specs=[row_out] * 5,
        out_shape=[jax.ShapeDtypeStruct((n, w), BF16), jax.ShapeDtypeStruct((n, w), F32),
                   jax.ShapeDtypeStruct((n, w), BF16), jax.ShapeDtypeStruct((n, w), BF16),
                   jax.ShapeDtypeStruct((n, w), BF16)],
        compiler_params=_cparams(("parallel",)),
        name="qk_norm",
    )(proj, proj, proj, proj, tile(gq), tile(gk), tile(gm),
      _group_mean_matrix(w, DA_DK), _group_mean_matrix(w, HEAD_W))


def _hgrn_kernel(*refs, layer, nb, tc, c, tvalid, has_s0):
    if has_s0:
        hq_ref, hf_ref, hi_ref, hg_ref, lbl_ref, og_ref, tri_ref, s0_ref = refs[:8]
        rest = refs[8:]
    else:
        hq_ref, hf_ref, hi_ref, hg_ref, lbl_ref, og_ref, tri_ref = refs[:7]
        s0_ref = None
        rest = refs[7:]
    o_ref, sout_ref, st_ref, q_s, k_s, b_s = rest
    t = pl.program_id(1)
    rows = nb * tc

    @pl.when(t == 0)
    def _():
        if has_s0:
            for ib in range(nb):
                for h in range(HEADS):
                    st_ref[ib, h] = s0_ref[ib, h].T
        else:
            st_ref[...] = jnp.zeros_like(st_ref)

    lg = lbl_ref[...]
    e = jnp.exp(lg - jnp.max(lg, axis=0, keepdims=True))
    pr = e / jnp.sum(e, axis=0, keepdims=True)
    lb = jnp.zeros((1, BRANCH_W), F32)
    for i in range(1, layer + 1):
        lb = lb + pr[i:i + 1]
    lb = jnp.clip(lb, 0.0, 1.0 - 1e-4)

    z = hf_ref[...]
    sg = _sigmoid(z)
    logf = jnp.log(lb + (1.0 - lb) * sg)
    kk = (1.0 - lb) * _sigmoid(-z)
    if tvalid < c:
        valid = (lax.broadcasted_iota(I32, (rows, BRANCH_W), 0) % c) < tvalid
        logf = jnp.where(valid, logf, 0.0)
        kk = jnp.where(valid, kk, 0.0)
    hq = hq_ref[...]
    q_s[...] = hq * _sigmoid(hq)
    k_s[...] = kk
    b_s[...] = _split_dot(logf, tri_ref[...], 3, left=True)

    og = og_ref[...]
    row_id = lax.broadcasted_iota(I32, (c, HEAD_W), 0)
    nsub = tc // c

    def sub(i, carry):
        r0 = pl.multiple_of(i * c, c)
        ib = i // nsub
        first = (i % nsub) == 0
        new_carry = []
        for h in range(HEADS):
            sl = slice(h * HEAD_W, (h + 1) * HEAD_W)
            bg = b_s[pl.ds(r0, c), sl]
            new_carry.append(bg[c - 1:c, :])
            b = bg - jnp.where(first, 0.0, carry[h])
            q = q_s[pl.ds(r0, c), sl]
            kx = k_s[pl.ds(r0, c), sl]
            iv = hi_ref[pl.ds(r0, c), sl]
            st = st_ref[ib, h]
            o_inter = lax.dot_general((q * jnp.exp(b)).astype(BF16), st.astype(BF16), NT_DIMS,
                                      preferred_element_type=F32)
            o_rows = []
            for tt in range(c):
                d = b[tt:tt + 1, :] - b
                w = jnp.exp(jnp.where(row_id <= tt, d, MASK_VALUE))
                p = (q[tt:tt + 1, :] * kx) * w
                a = jnp.sum(p, axis=-1, keepdims=True)
                o_rows.append(jnp.sum(a * iv, axis=0, keepdims=True))
            o = o_inter + jnp.concatenate(o_rows, axis=0)
            bl = b[c - 1:c, :]
            kd = kx * jnp.exp(bl - b)
            upd = lax.dot_general(iv.astype(BF16), kd.astype(BF16), TN_DIMS, preferred_element_type=F32)
            st_ref[ib, h] = jnp.exp(bl) * st + upd
            ms = jnp.mean(o * o, axis=-1, keepdims=True)
            hg = hg_ref[pl.ds(r0, c), sl]
            o_ref[pl.ds(r0, c), sl] = (((o * lax.rsqrt(ms + EPS)) * og) * (hg * _sigmoid(hg))).astype(o_ref.dtype)
        return tuple(new_carry)

    lax.fori_loop(0, nb * nsub, sub, tuple(jnp.zeros((1, HEAD_W), F32) for _ in range(HEADS)))

    @pl.when(t == pl.num_programs(1) - 1)
    def _():
        for ib in range(nb):
            for h in range(HEADS):
                sout_ref[ib, h] = st_ref[ib, h].T


def _hgrn(src, lb_logits, onorm_g, s0_all, *, layer, n_seq, seq_len, rows_out, row0, nb, tc, c, tvalid,
          out_dtype):
    rows = nb * tc
    nt = seq_len // tc
    rb0 = row0 // rows
    w = BRANCH_W
    col = lambda cc: pl.BlockSpec((rows, w), lambda b, t, cc=cc: (rb0 + b * nt + t, cc))
    r = np.arange(rows)
    tri = jnp.asarray((r[:, None] >= r[None, :]) & (r[:, None] // tc == r[None, :] // tc), BF16)
    in_specs = [col(0), col(1), col(2), col(3),
                pl.BlockSpec((DEPTH, w), lambda b, t: (0, 0)),
                pl.BlockSpec((1, HEAD_W), lambda b, t: (0, 0)),
                pl.BlockSpec((rows, rows), lambda b, t: (0, 0))]
    args = [src, src, src, src, lb_logits, onorm_g.reshape(1, HEAD_W), tri]
    has_s0 = s0_all is not None
    if has_s0:
        in_specs.append(pl.BlockSpec((None, nb, HEADS, HEAD_W, HEAD_W), lambda b, t: (layer, b, 0, 0, 0)))
        args.append(s0_all)
    kern = functools.partial(_hgrn_kernel, layer=layer, nb=nb, tc=tc, c=c, tvalid=tvalid, has_s0=has_s0)
    return pl.pallas_call(
        kern,
        grid=(n_seq // nb, nt),
        in_specs=in_specs,
        out_specs=[pl.BlockSpec((rows, w), lambda b, t: (b * nt + t, 0)),
                   pl.BlockSpec((nb, HEADS, HEAD_W, HEAD_W), lambda b, t: (b, 0, 0, 0))],
        out_shape=[jax.ShapeDtypeStruct((rows_out, w), out_dtype),
                   jax.ShapeDtypeStruct((n_seq, HEADS, HEAD_W, HEAD_W), F32)],
        scratch_shapes=[pltpu.VMEM((nb, HEADS, HEAD_W, HEAD_W), F32),
                        pltpu.VMEM((rows, w), F32), pltpu.VMEM((rows, w), F32), pltpu.VMEM((rows, w), F32)],
        compiler_params=_cparams(("parallel", "arbitrary")),
        name="hgrn2",
    )(*args)


def _lambda_value(lp, lam_init):
    a = jnp.sum(lp[0:1] * lp[1:2], axis=-1, keepdims=True)
    b = jnp.sum(lp[2:3] * lp[3:4], axis=-1, keepdims=True)
    return jnp.exp(a) - jnp.exp(b) + lam_init


def _dattn_prompt_kernel(iq_tab, ik_tab, lam_ref, q_ref, k_ref, v_ref, sg_ref, o_ref, q2_s, m_s, acc_s,
                         *, tq, rc, lam_init):
    step = pl.program_id(2)
    iq = iq_tab[step]
    ik = ik_tab[step]

    @pl.when(ik == 0)
    def _():
        q = q_ref[...].astype(F32)
        lane = lax.broadcasted_iota(I32, q.shape, 1)
        q2_s[0:tq, :] = jnp.where(lane < DA_DK, q, 0.0).astype(BF16)
        q2_s[tq:2 * tq, :] = jnp.where(lane >= DA_DK, q, 0.0).astype(BF16)
        m_s[...] = jnp.full_like(m_s, NEG_INF)
        acc_s[...] = jnp.zeros_like(acc_s)

    def update(masked):
        v1 = jnp.concatenate([v_ref[...], jnp.ones((tq, LANES), BF16)], axis=1)
        for c0 in range(0, 2 * tq, rc):
            rows = slice(c0, c0 + rc)
            nk = (c0 % tq) + rc if masked else tq
            s = lax.dot_general(q2_s[rows, :], k_ref[0:nk, :], NT_DIMS, preferred_element_type=F32)
            if masked:
                r = lax.broadcasted_iota(I32, s.shape, 0) + (c0 % tq)
                cidx = lax.broadcasted_iota(I32, s.shape, 1)
                s = jnp.where(cidx <= r, s, MASK_VALUE)
            m_prev = m_s[rows, :]
            m_new = jnp.maximum(m_prev, jnp.max(s, axis=-1, keepdims=True))
            alpha = jnp.exp2(m_prev - m_new)
            p = jnp.exp2(s - jnp.tile(m_new, (1, nk // LANES)))
            acc_s[rows, :] = (jnp.tile(alpha, (1, 2)) * acc_s[rows, :]
                              + jnp.dot(p.astype(BF16), v1[0:nk, :], preferred_element_type=F32))
            m_s[rows, :] = m_new

    @pl.when(ik < iq)
    def _():
        update(False)

    @pl.when(ik == iq)
    def _():
        update(True)
        lam = _lambda_value(lam_ref[...], lam_init)
        o0 = acc_s[0:tq, 0:HEAD_W] / acc_s[0:tq, HEAD_W:]
        o1 = acc_s[tq:2 * tq, 0:HEAD_W] / acc_s[tq:2 * tq, HEAD_W:]
        da = o0 - lam * o1
        ms = jnp.mean(da * da, axis=-1, keepdims=True)
        o_ref[...] = (((da * lax.rsqrt(ms + EPS)) * sg_ref[...]) * (1.0 - lam_init)).astype(o_ref.dtype)


def _dattn_prompt(qn, knb, vb, lam_p, subln_g, *, n_seq, seq_len, rows_out, tq, rc, lam_init):
    nq = seq_len // tq
    pairs = [(iq, ik) for iq in range(nq) for ik in range(iq + 1)]
    iq_tab = jnp.asarray([p[0] for p in pairs], I32)
    ik_tab = jnp.asarray([p[1] for p in pairs], I32)
    qspec = pl.BlockSpec((tq, HEAD_W), lambda b, h, s, iqt, ikt: (b * nq + iqt[s], h))
    kspec = pl.BlockSpec((tq, HEAD_W), lambda b, h, s, iqt, ikt: (b * nq + ikt[s], h))
    grid_spec = pltpu.PrefetchScalarGridSpec(
        num_scalar_prefetch=2,
        grid=(n_seq, HEADS, len(pairs)),
        in_specs=[pl.BlockSpec((4, DA_DK), lambda b, h, s, iqt, ikt: (0, 0)), qspec, kspec, kspec,
                  pl.BlockSpec((1, HEAD_W), lambda b, h, s, iqt, ikt: (0, 0))],
        out_specs=qspec,
        scratch_shapes=[pltpu.VMEM((2 * tq, HEAD_W), BF16), pltpu.VMEM((2 * tq, LANES), F32),
                        pltpu.VMEM((2 * tq, HEAD_W + LANES), F32)],
    )
    return pl.pallas_call(
        functools.partial(_dattn_prompt_kernel, tq=tq, rc=rc, lam_init=lam_init),
        grid_spec=grid_spec,
        out_shape=jax.ShapeDtypeStruct((rows_out, BRANCH_W), BF16),
        compiler_params=_cparams(("parallel", "parallel", "arbitrary")),
        name="diff_attn_prompt",
    )(iq_tab, ik_tab, lam_p, qn, knb, vb, subln_g.reshape(1, HEAD_W))


def _dattn_sample_kernel(pt_ref, lam_ref, qr_ref, kc_ref, vc_ref, sg_ref, *rest, n_pages, t_new, lam_init):
    k_refs = rest[:n_pages]
    v_refs = rest[n_pages:2 * n_pages]
    o_ref = rest[2 * n_pages]
    nrow = 2 * HEADS * t_new
    qr = qr_ref[...]
    s_past = jnp.concatenate(
        [jnp.dot(qr, k[...].astype(BF16), preferred_element_type=F32) for k in k_refs], axis=1)

    qf = qr.astype(F32)
    kc = kc_ref[...]
    vc = vc_ref[...]
    t_of_row = lax.broadcasted_iota(I32, (nrow, 1), 0) % t_new
    s_new = []
    for j in range(t_new):
        sj = jnp.sum(qf * kc[j:j + 1, :], axis=-1, keepdims=True)
        s_new.append(jnp.where(t_of_row >= j, sj, MASK_VALUE))
    m = jnp.max(s_past, axis=-1, keepdims=True)
    for sj in s_new:
        m = jnp.maximum(m, sj)
    p_past = jnp.exp2(s_past - m)
    p_new = [jnp.exp2(sj - m) for sj in s_new]
    l = jnp.sum(p_past, axis=-1, keepdims=True)
    for pj in p_new:
        l = l + pj

    lam = _lambda_value(lam_ref[...], lam_init)
    rows_h = 2 * t_new
    for h in range(HEADS):
        sl = slice(h * HEAD_W, (h + 1) * HEAD_W)
        r0 = h * rows_h
        v_h = jnp.concatenate([v[pl.ds(h, PAGE_SIZE, stride=HEADS), :] for v in v_refs], axis=0)
        acc = jnp.dot(p_past[r0:r0 + rows_h].astype(BF16), v_h.astype(BF16), preferred_element_type=F32)
        for j in range(t_new):
            acc = acc + p_new[j][r0:r0 + rows_h] * vc[j:j + 1, sl]
        on = acc / l[r0:r0 + rows_h]
        da = on[:t_new] - lam * on[t_new:]
        ms = jnp.mean(da * da, axis=-1, keepdims=True)
        o_ref[:, sl] = (((da * lax.rsqrt(ms + EPS)) * sg_ref[...]) * (1.0 - lam_init)).astype(o_ref.dtype)


def _dattn_sample(page_table, qrows, kcur, vcur, cache_kt, cache_v, lam_p, subln_g, *, layer, lam_init):
    n_seq, n_pages = page_table.shape
    t_new = kcur.shape[1]
    nrow = qrows.shape[1]
    per_seq = lambda shape: pl.BlockSpec((None,) + shape, lambda b, pt: (b, 0, 0))
    page = lambda p: pl.BlockSpec((None, None, PAGE_SIZE * HEADS, HEAD_W),
                                  lambda b, pt, p=p: (layer, pt[b * n_pages + p], 0, 0))
    grid_spec = pltpu.PrefetchScalarGridSpec(
        num_scalar_prefetch=1,
        grid=(n_seq,),
        in_specs=[pl.BlockSpec((4, DA_DK), lambda b, pt: (0, 0)),
                  per_seq((nrow, BRANCH_W)), per_seq((t_new, BRANCH_W)), per_seq((t_new, BRANCH_W)),
                  pl.BlockSpec((1, HEAD_W), lambda b, pt: (0, 0))]
                 + [page(p) for p in range(n_pages)] * 2,
        out_specs=per_seq((t_new, BRANCH_W)),
    )
    return pl.pallas_call(
        functools.partial(_dattn_sample_kernel, n_pages=n_pages, t_new=t_new, lam_init=lam_init),
        grid_spec=grid_spec,
        out_shape=jax.ShapeDtypeStruct((n_seq, t_new, BRANCH_W), F32),
        compiler_params=_cparams(("parallel",), vmem_mb=40),
        name="diff_attn_sample",
    )(page_table.reshape(-1), lam_p, qrows, kcur, vcur, subln_g.reshape(1, HEAD_W),
      *([cache_kt] * n_pages), *([cache_v] * n_pages))


def _mem_kv_kernel(m_ref, g_ref, w_ref, kg_ref, bd_ref, mk_ref, mv_ref):
    x = m_ref[...]
    ms = jnp.mean(x * x, axis=-1, keepdims=True)
    hn = ((x * lax.rsqrt(ms + EPS)) * g_ref[...]).astype(BF16)
    kv = jnp.dot(hn, w_ref[...], preferred_element_type=F32)
    k = kv[:, :BRANCH_W]
    ms = _split_dot(k * k, bd_ref[...], 3)
    mk_ref[...] = (k * lax.rsqrt(ms + EPS)) * kg_ref[...]
    mv_ref[...] = kv[:, BRANCH_W:]


def _mem_kv(mem, norm_g, w_kv, knorm_g):
    n_seq = mem.shape[0]
    w = BRANCH_W
    out = pl.BlockSpec((None, None, MEM_LEN, w), lambda l, b: (l, b, 0, 0))
    return pl.pallas_call(
        _mem_kv_kernel,
        grid=(DEPTH, n_seq),
        in_specs=[pl.BlockSpec((None, MEM_LEN, D_MODEL), lambda l, b: (b, 0, 0)),
                  pl.BlockSpec((None, 1, D_MODEL), lambda l, b: (l, 0, 0)),
                  pl.BlockSpec((None, D_MODEL, 2 * w), lambda l, b: (l, 0, 0)),
                  pl.BlockSpec((None, 1, w), lambda l, b: (l, 0, 0)),
                  pl.BlockSpec((w, w), lambda l, b: (0, 0))],
        out_specs=[out, out],
        out_shape=[jax.ShapeDtypeStruct((DEPTH, n_seq, MEM_LEN, w), F32)] * 2,
        compiler_params=_cparams(("parallel", "parallel")),
        name="mem_kv",
    )(mem, norm_g.reshape(DEPTH, 1, D_MODEL), w_kv,
      jnp.tile(knorm_g.astype(F32), (1, HEADS)).reshape(DEPTH, 1, w), _group_mean_matrix(w, HEAD_W))


def _cross_kernel(q_ref, mk_ref, mv_ref, o_ref, *, head_rows, n_seq_blk):
    tq = q_ref.shape[0] // n_seq_blk
    for r in range(n_seq_blk):
        q = q_ref[r * tq:(r + 1) * tq, :]
        for h in range(HEADS):
            sl = slice(h * HEAD_W, (h + 1) * HEAD_W)
            if head_rows:
                m0 = r * MEM_LEN * HEADS + h
                mk = mk_ref[pl.ds(m0, MEM_LEN, stride=HEADS), :].astype(BF16)
                mv = mv_ref[pl.ds(m0, MEM_LEN, stride=HEADS), :].astype(BF16)
            else:
                mk = mk_ref[r * MEM_LEN:(r + 1) * MEM_LEN, sl].astype(BF16)
                mv = mv_ref[r * MEM_LEN:(r + 1) * MEM_LEN, sl].astype(BF16)
            s = lax.dot_general(q[:, sl], mk, NT_DIMS, preferred_element_type=F32) * (HEAD_W ** -0.5)
            p = jnp.exp(s - jnp.max(s, axis=-1, keepdims=True))
            l = jnp.sum(p, axis=-1, keepdims=True)
            o = jnp.dot(p.astype(BF16), mv, preferred_element_type=F32) / l
            o_ref[r * tq:(r + 1) * tq, sl] = o.astype(o_ref.dtype)


def _cross_prompt(qm, mk_all, mv_all, *, layer, n_seq, seq_len, rows_out, tq):
    nq = seq_len // tq
    mem = pl.BlockSpec((None, None, MEM_LEN, BRANCH_W), lambda b, i: (layer, b, 0, 0))
    row = pl.BlockSpec((tq, BRANCH_W), lambda b, i: (b * nq + i, 0))
    return pl.pallas_call(
        functools.partial(_cross_kernel, head_rows=False, n_seq_blk=1),
        grid=(n_seq, nq),
        in_specs=[row, mem, mem],
        out_specs=row,
        out_shape=jax.ShapeDtypeStruct((rows_out, BRANCH_W), BF16),
        compiler_params=_cparams(("parallel", "parallel")),
        name="cross_attn_prompt",
    )(qm, mk_all, mv_all)


def _cross_sample(qm_pad, mk_all, mv_all, *, layer):
    n_seq, tpad, w = qm_pad.shape
    nb = CROSS_SAMPLE_SEQS
    depth = mk_all.shape[0]
    mem_rows = nb * MEM_LEN * HEADS
    mem = pl.BlockSpec((None, None, mem_rows, HEAD_W), lambda b: (layer, b, 0, 0))
    row = pl.BlockSpec((None, nb * tpad, w), lambda b: (b, 0, 0))
    grouped = lambda m: m.reshape(depth, n_seq // nb, mem_rows, HEAD_W)
    return pl.pallas_call(
        functools.partial(_cross_kernel, head_rows=True, n_seq_blk=nb),
        grid=(n_seq // nb,),
        in_specs=[row, mem, mem],
        out_specs=row,
        out_shape=jax.ShapeDtypeStruct((n_seq // nb, nb * tpad, w), F32),
        compiler_params=_cparams(("parallel",)),
        name="cross_attn_sample",
    )(qm_pad.reshape(n_seq // nb, nb * tpad, w), grouped(mk_all), grouped(mv_all)).reshape(n_seq, tpad, w)


def _merge_kernel(x_ref, g0_ref, g1_ref, g2_ref, a0_ref, a1_ref, a2_ref, w0_ref, w1_ref, w2_ref, wo_ref, o_ref):
    merged = None
    for g_ref, a_ref, w_ref in ((g0_ref, a0_ref, w0_ref), (g1_ref, a1_ref, w1_ref), (g2_ref, a2_ref, w2_ref)):
        y = _sigmoid(g_ref[...]) * jnp.dot(a_ref[...], w_ref[...], preferred_element_type=F32)
        merged = y if merged is None else merged + y
    o_ref[...] = x_ref[...] + jnp.dot(merged.astype(BF16), wo_ref[...], preferred_element_type=F32)


def _merge(x, proj, hg, da, cm, w_hg, w_da, w_mem, w_out, *, tn):
    n, d = x.shape
    w = BRANCH_W
    gate_col0 = (8 * w) // d
    row = lambda width: pl.BlockSpec((tn, width), lambda i: (i, 0))
    gate = lambda c: pl.BlockSpec((tn, d), lambda i, c=c: (i, gate_col0 + c))
    const = lambda shape: pl.BlockSpec(shape, lambda i: (0, 0))
    return pl.pallas_call(
        _merge_kernel,
        grid=(n // tn,),
        in_specs=[row(d), gate(0), gate(1), gate(2), row(w), row(w), row(w),
                  const((w, d)), const((w, d)), const((w, d)), const((d, d))],
        out_specs=row(d),
        out_shape=jax.ShapeDtypeStruct((n, d), F32),
        compiler_params=_cparams(("parallel",), vmem_mb=48),
        name="branch_merge",
    )(x, proj, proj, proj, hg, da, cm, w_hg, w_da, w_mem, w_out)


def _topk_rows(scores, k, key_id):
    scores = list(scores)
    vals = [[] for _ in scores]
    ids = [[] for _ in scores]
    for _ in range(k):
        for n, s in enumerate(scores):
            m = jnp.max(s, axis=0, keepdims=True)
            sel = jnp.min(jnp.where(s == m, key_id, _ID_SENTINEL), axis=0, keepdims=True)
            scores[n] = jnp.where(key_id == sel, NEG_INF, s)
            vals[n].append(m)
            ids[n].append(sel)
        yield None
    return vals, ids


def _candidate_tables(tt):
    flat = np.full((_CAND_ROWS,), _ID_SENTINEL, np.float64)
    flat[0:16] = np.arange(16)
    flat[16:24] = 16 + np.arange(8)
    for j1 in range(2, 8):
        lim = PK_TOPK // (j1 + 1)
        flat[24 + (j1 - 2) * 8:24 + (j1 - 2) * 8 + lim] = j1 * 16 + np.arange(lim)
    flat[72:80] = (8 + np.arange(8)) * 16
    return jnp.asarray(np.broadcast_to(flat[:, None], (_CAND_ROWS, tt)), F32)


def _norm_bf16(x, g):
    ms = jnp.mean(x * x, axis=-1, keepdims=True)
    return ((x * lax.rsqrt(ms + EPS)) * g).astype(BF16)


def _select_head_steps(h, hn, wq_ref, sk_ref, cflat, eid_t, gate_t, lane0=0):
    tt = hn.shape[0]
    cflat = cflat[:, :tt]
    key_id = lax.broadcasted_iota(I32, (N_KEYS, tt), 0).astype(F32)
    cvalid = cflat < float(PK_TOPK * PK_TOPK)
    qh = jnp.dot(hn, wq_ref[h], preferred_element_type=F32).astype(BF16)
    sc = lax.dot_general(sk_ref[h], qh, NT_DIMS, preferred_element_type=F32)
    yield None
    (v1, v2), (i1, i2) = yield from _topk_rows([sc[:N_KEYS], sc[N_KEYS:]], PK_TOPK, key_id)
    v2a = jnp.concatenate(v2, axis=0)
    i2a = jnp.concatenate(i2, axis=0)
    pieces_v = [v1[0] + v2a, v1[1] + v2a[:8]]
    pieces_e = [i1[0] * N_KEYS + i2a, i1[1] * N_KEYS + i2a[:8]]
    for j1 in range(2, 8):
        pieces_v.append(v1[j1] + v2a[:8])
        pieces_e.append(i1[j1] * N_KEYS + i2a[:8])
    pieces_v.append(jnp.concatenate(v1[8:], axis=0) + v2a[:1])
    pieces_e.append(jnp.concatenate(i1[8:], axis=0) * N_KEYS + i2a[:1])
    cand = jnp.where(cvalid, jnp.concatenate(pieces_v, axis=0), NEG_INF)
    ce = jnp.concatenate(pieces_e, axis=0)
    fv, fe = [], []
    for _ in range(PK_TOPK):
        m = jnp.max(cand, axis=0, keepdims=True)
        sel = jnp.min(jnp.where(cand == m, cflat, _ID_SENTINEL), axis=0, keepdims=True)
        hit = cflat == sel
        fe.append(jnp.max(jnp.where(hit, ce, -1.0), axis=0, keepdims=True))
        cand = jnp.where(hit, NEG_INF, cand)
        fv.append(m)
        yield None
    fva = jnp.concatenate(fv, axis=0)
    ex = jnp.exp(fva - fva[0:1])
    r0 = pl.multiple_of(h * PK_TOPK, PK_TOPK)
    gate_t[pl.ds(r0, PK_TOPK), lane0:lane0 + tt] = ex / jnp.sum(ex, axis=0, keepdims=True)
    eid_t[pl.ds(r0, PK_TOPK), lane0:lane0 + tt] = jnp.concatenate(fe, axis=0)


def _select_head(*args):
    for _ in _select_head_steps(*args):
        pass


def _build_routing(eid_s, gate_s, w_s, *, tn, pitch):
    n_sel = PK_HEADS * PK_TOPK
    row_id = lax.broadcasted_iota(I32, (N_KEYS, n_sel), 0).astype(F32).astype(BF16)
    one = jnp.ones((N_KEYS, n_sel), BF16)
    zero = jnp.zeros((N_KEYS, n_sel), BF16)
    bf16_rows = 16

    def all_rows(v):
        return jnp.tile(jnp.broadcast_to(v, (bf16_rows, n_sel)).astype(BF16), (N_KEYS // bf16_rows, 1))

    group = 64

    def build(n8, carry):
        n0 = pl.multiple_of(n8 * group, group)
        e8 = eid_s[pl.ds(n0, group), :]
        g8 = gate_s[pl.ds(n0, group), :]
        for r in range(group):
            e = e8[r:r + 1, :]
            g = 0.5 * g8[r:r + 1, :]
            a_id = jnp.floor(e * (1.0 / N_KEYS))
            sel_a = jnp.where(row_id == all_rows(a_id), one, zero)
            hit_b = row_id == all_rows(e - a_id * N_KEYS)
            g_hi = g.astype(BF16).astype(F32)
            g_lo = g - g_hi
            lhs = jnp.concatenate([sel_a, sel_a], axis=1)
            rhs = jnp.concatenate([jnp.where(hit_b, all_rows(g_hi), zero),
                                   jnp.where(hit_b, all_rows(g_lo), zero)], axis=1)
            w = lax.dot_general(lhs, rhs, NT_DIMS, preferred_element_type=F32)
            w_s[pl.ds(n0 + r, N_KEYS, stride=pitch), :] = w
        return carry

    lax.fori_loop(0, tn // group, build, 0)


def _peer_kernel(xc_ref, xn_ref, g_ref, wq_ref, sk_ref, cflat_ref, u_ref, v_ref, o_ref,
                 w_s, acc_s, hn_s, eid_t, gate_t, eid_s, gate_s, *, tn, te, pitch):
    i = pl.program_id(0)
    j = pl.program_id(1)
    cur = i % 2
    nxt = 1 - cur
    cflat = cflat_ref[...]

    @pl.when(j == 0)
    def _():
        @pl.when(i == 0)
        def _():
            hn0 = _norm_bf16(xc_ref[...], g_ref[...])
            hn_s[0] = hn0

            def head(h, carry):
                _select_head(h, hn0, wq_ref, sk_ref, cflat, eid_t, gate_t)
                return carry

            lax.fori_loop(0, PK_HEADS, head, 0)
            eid_s[...] = eid_t[...].T
            gate_s[...] = gate_t[...].T

        _build_routing(eid_s, gate_s, w_s, tn=tn, pitch=pitch)
        acc_s[...] = jnp.zeros_like(acc_s)
        hn_s[nxt] = _norm_bf16(xn_ref[...], g_ref[...])

    sub = LANES
    hn_next = hn_s[nxt]
    selects = [_select_head_steps(j, hn_next[t0:t0 + sub], wq_ref, sk_ref, cflat, eid_t, gate_t, lane0=t0)
               for t0 in range(0, tn, sub)]

    def advance(units):
        for _ in range(units):
            while selects and next(selects[0], "done") == "done":
                selects.pop(0)
                if not selects:
                    return

    advance(1)
    units_per_piece = 8
    hn = hn_s[cur]
    acc = acc_s[...]
    e_chunk = 2 * LANES
    v_chunk = 8 * e_chunk
    for c0 in range(0, te, v_chunk):
        pieces = []
        for c1 in range(c0, c0 + v_chunk, e_chunk):
            a = lax.dot_general(hn, u_ref[c1:c1 + e_chunk, :], NT_DIMS, preferred_element_type=F32)
            advance(units_per_piece // 2)
            for r in range(e_chunk // LANES):
                ar = a[:, r * LANES:(r + 1) * LANES]
                a_id = j * (te // LANES) + c1 // LANES + r
                wr = w_s[pl.ds(pl.multiple_of(a_id * pitch, 8), tn), :]
                act = ar * (1.0 + lax.erf(ar * (0.5 ** 0.5)))
                pieces.append((wr * act).astype(BF16))
            advance(units_per_piece - units_per_piece // 2)
        acc = acc + jnp.dot(jnp.concatenate(pieces, axis=1), v_ref[c0:c0 + v_chunk, :],
                            preferred_element_type=F32)
        advance(units_per_piece)
    acc_s[...] = acc
    advance(len(selects) * 64)

    @pl.when(j == pl.num_programs(1) - 1)
    def _():
        o_ref[...] = xc_ref[...] + acc_s[...]
        eid_s[...] = eid_t[...].T
        gate_s[...] = gate_t[...].T


def _peer(x, g, wq_heads, sk_pad, u_all, v_all, *, layer, tn, te):
    n, d = x.shape
    nt = n // tn
    assert N_EXPERTS // te == PK_HEADS
    sel = PK_HEADS * PK_TOPK
    pitch = tn + 8
    table = pl.BlockSpec((None, te, d), lambda i, j: (layer, j, 0))
    const = lambda shape: pl.BlockSpec(shape, lambda i, j: (0,) * len(shape))
    return pl.pallas_call(
        functools.partial(_peer_kernel, tn=tn, te=te, pitch=pitch),
        grid=(nt, N_EXPERTS // te),
        in_specs=[pl.BlockSpec((tn, d), lambda i, j: (i, 0)),
                  pl.BlockSpec((tn, d), lambda i, j: (jnp.minimum(i + 1, nt - 1), 0)),
                  const((1, d)), const((PK_HEADS, d, LANES)), const((PK_HEADS, 2 * N_KEYS, LANES)),
                  const((_CAND_ROWS, tn)), table, table],
        out_specs=pl.BlockSpec((tn, d), lambda i, j: (i, 0)),
        out_shape=jax.ShapeDtypeStruct((n, d), F32),
        scratch_shapes=[pltpu.VMEM((N_KEYS * pitch, LANES), F32), pltpu.VMEM((tn, d), F32),
                        pltpu.VMEM((2, tn, d), BF16),
                        pltpu.VMEM((sel, tn), F32), pltpu.VMEM((sel, tn), F32),
                        pltpu.VMEM((tn, sel), F32), pltpu.VMEM((tn, sel), F32)],
        compiler_params=_cparams(("arbitrary", "arbitrary"), vmem_mb=60),
        name="peer",
    )(x, x, g.reshape(1, d), wq_heads, sk_pad, _candidate_tables(tn), u_all, v_all)


def _pad_tokens(a, tpad):
    return jnp.pad(a, ((0, 0), (0, tpad - a.shape[1]), (0, 0)))


def kernel(x_prompt, x_sample, cache_diff_k, cache_diff_v, cache_mem_k, cache_mem_v, state_hgrn, page_table, mem_prompt, norm_mix_g, w_in, hg_lb_logits, hg_onorm_g, da_qnorm_g, da_knorm_g, da_lambda, da_subln_g, mem_norm_g, w_mem_kv, mem_qnorm_g, mem_knorm_g, w_branch_hg, w_branch_da, w_branch_mem, w_out, norm_ffn_g, peer_wq, peer_subkeys, peer_u, peer_v):
    n_p, seq, d = x_prompt.shape
    n_s, t_new, _ = x_sample.shape
    rows_p = n_p * seq
    rows_s = n_s * t_new
    w = BRANCH_W
    depth = w_in.shape[0]

    w_in_b = w_in.astype(BF16)
    w_kv_b = w_mem_kv.astype(BF16)
    w_hg_b, w_da_b, w_cm_b, w_out_b = (a.astype(BF16) for a in (w_branch_hg, w_branch_da, w_branch_mem, w_out))
    wq_heads = peer_wq.astype(BF16).reshape(depth, d, PK_HEADS, LANES).transpose(0, 2, 1, 3)
    sk = peer_subkeys.astype(BF16)
    zk = jnp.zeros_like(sk[:, :, 0])
    sk_pad = jnp.concatenate([jnp.concatenate([sk[:, :, 0], zk], axis=-1),
                              jnp.concatenate([zk, sk[:, :, 1]], axis=-1)], axis=2)
    u_b = peer_u.astype(BF16)
    v_b = peer_v.astype(BF16)
    n_pool = cache_diff_k.shape[1]
    cache_kt = jnp.transpose(cache_diff_k, (0, 1, 3, 4, 5, 2)).reshape(depth, n_pool, w, PAGE_SIZE)
    cache_v = cache_diff_v.reshape(depth, n_pool, PAGE_SIZE * HEADS, HEAD_W)
    cmem_k = cache_mem_k.reshape(depth, n_s, MEM_LEN * HEADS, HEAD_W)
    cmem_v = cache_mem_v.reshape(depth, n_s, MEM_LEN * HEADS, HEAD_W)
    lb_logits = hg_lb_logits.astype(F32)
    rows_all = rows_p + rows_s
    with_sample = lambda full, part: lax.dynamic_update_slice(full, part, (rows_p, 0))

    mk_p, mv_p = _mem_kv(mem_prompt, mem_norm_g, w_kv_b, mem_knorm_g)

    x = jnp.concatenate([x_prompt.reshape(rows_p, d), x_sample.reshape(rows_s, d)], axis=0)
    row_mask = (np.arange(2 * HEADS * t_new)[:, None] // t_new) == (np.arange(w)[None, :] // DA_DK)
    row_mask = jnp.asarray(row_mask)
    tpad_h = 8
    tpad_c = 16

    hg_p, hg_s, dk_p, dv_p, dk_s, dv_s = [], [], [], [], [], []
    for l in range(depth):
        lam_init = 0.8 - 0.6 * math.exp(-0.3 * l)
        lam_p = da_lambda[l].astype(F32)
        proj = _norm_matmul(x, norm_mix_g[l], w_in_b[l], tn=PROJ_ROW_TILE, tc=PROJ_COL_TILE)
        qn, kn, knb, vb, qm = _prep(proj, da_qnorm_g[l], da_knorm_g[l], mem_qnorm_g[l], tn=ROW_TILE)

        hgo_p, st_p = _hgrn(proj, lb_logits, hg_onorm_g[l], None, layer=l, n_seq=n_p, seq_len=seq,
                            rows_out=rows_all, row0=0, nb=1, tc=HGRN_BLOCK, c=HGRN_CHUNK, tvalid=HGRN_CHUNK,
                            out_dtype=BF16)
        src_s = _pad_tokens(proj[rows_p:, :4 * w].reshape(n_s, t_new, 4 * w), tpad_h).reshape(n_s * tpad_h, 4 * w)
        hgo_s, st_s = _hgrn(src_s, lb_logits, hg_onorm_g[l], state_hgrn, layer=l, n_seq=n_s, seq_len=tpad_h,
                            rows_out=n_s * tpad_h, row0=0, nb=8, tc=tpad_h, c=tpad_h, tvalid=t_new, out_dtype=F32)
        hgo_s = hgo_s.reshape(n_s, tpad_h, w)[:, :t_new].reshape(rows_s, w).astype(BF16)

        da_p = _dattn_prompt(qn, knb, vb, lam_p, da_subln_g[l], n_seq=n_p, seq_len=seq, rows_out=rows_all,
                             tq=ATTN_TILE, rc=ATTN_ROW_CHUNK, lam_init=lam_init)
        qs = qn[rows_p:].reshape(n_s, t_new, w)
        qrows = jnp.where(row_mask, jnp.tile(qs, (1, 2 * HEADS, 1)), jnp.zeros((), BF16))
        da_s = _dattn_sample(page_table, qrows, knb[rows_p:].reshape(n_s, t_new, w).astype(F32),
                             vb[rows_p:].reshape(n_s, t_new, w).astype(F32), cache_kt, cache_v, lam_p,
                             da_subln_g[l], layer=l, lam_init=lam_init)
        da_s = da_s.reshape(rows_s, w).astype(BF16)

        cm_p = _cross_prompt(qm, mk_p, mv_p, layer=l, n_seq=n_p, seq_len=seq, rows_out=rows_all, tq=ATTN_TILE)
        qm_s = _pad_tokens(qm[rows_p:].reshape(n_s, t_new, w), tpad_c)
        cm_s = _cross_sample(qm_s, cmem_k, cmem_v, layer=l)[:, :t_new].reshape(rows_s, w).astype(BF16)

        x = _merge(x, proj, with_sample(hgo_p, hgo_s), with_sample(da_p, da_s), with_sample(cm_p, cm_s),
                   w_hg_b[l], w_da_b[l], w_cm_b[l], w_out_b[l], tn=ROW_TILE)

        x = _peer(x, norm_ffn_g[l], wq_heads[l], sk_pad[l], u_b, v_b, layer=l,
                  tn=PEER_TOKEN_TILE, te=PEER_EXPERT_TILE)

        hg_p.append(st_p)
        hg_s.append(st_s)
        dk_p.append(kn[:rows_p].reshape(n_p, seq, HEADS, 2, DA_DK))
        dv_p.append(proj[:rows_p, 6 * w:7 * w].reshape(n_p, seq, HEADS, HEAD_W))
        dk_s.append(kn[rows_p:].reshape(n_s, t_new, HEADS, 2, DA_DK))
        dv_s.append(proj[rows_p:, 6 * w:7 * w].reshape(n_s, t_new, HEADS, HEAD_W))

    mem_shape = (depth, n_p, MEM_LEN, HEADS, HEAD_W)
    return (x[:rows_p].reshape(n_p, seq, d), x[rows_p:].reshape(n_s, t_new, d),
            jnp.stack(hg_p), jnp.stack(hg_s), jnp.stack(dk_p), jnp.stack(dv_p),
            jnp.stack(dk_s), jnp.stack(dv_s), mk_p.reshape(mem_shape), mv_p.reshape(mem_shape))
```

```python
import functools
import math

import numpy as np
import jax
import jax.numpy as jnp
from jax import lax
from jax.experimental import pallas as pl
from jax.experimental.pallas import tpu as pltpu

F32 = jnp.float32
BF16 = jnp.bfloat16
I32 = jnp.int32

D_MODEL = 1024
DEPTH = 4
PAGE_SIZE = 128
HEADS = 4
HEAD_W = 128
BRANCH_W = HEADS * HEAD_W
DA_DK = 64
MEM_LEN = 256
N_BRANCH = 3
IN_COLS = 8 * BRANCH_W + N_BRANCH * D_MODEL
PK_HEADS = 8
N_KEYS = 128
N_EXPERTS = N_KEYS * N_KEYS
PK_TOPK = 16
EPS = 1e-6
MASK_VALUE = -1e30
NEG_INF = float("-inf")
LOG2_E = math.log2(math.e)

LANES = 128

ROW_TILE = 512
PROJ_ROW_TILE = 1536
PROJ_COL_TILE = 1024
ATTN_TILE = 2048
ATTN_ROW_CHUNK = 512
HGRN_BLOCK = 512
HGRN_CHUNK = 16
CROSS_SAMPLE_SEQS = 4
PEER_TOKEN_TILE = 256
PEER_EXPERT_TILE = 2048
NT_DIMS = (((1,), (1,)), ((), ()))
TN_DIMS = (((0,), (0,)), ((), ()))

_CAND_ROWS = 80
_ID_SENTINEL = float(1 << 20)


def _cparams(sem, vmem_mb=None):
    kw = dict(dimension_semantics=sem)
    if vmem_mb is not None:
        kw["vmem_limit_bytes"] = vmem_mb * 2 ** 20
    return pltpu.CompilerParams(**kw)


def _split_dot(x, m_bf16, terms, left=False):
    acc = None
    r = x
    for _ in range(terms):
        p = r.astype(BF16)
        r = r - p.astype(F32)
        if left:
            y = jnp.dot(m_bf16, p, preferred_element_type=F32)
        else:
            y = jnp.dot(p, m_bf16, preferred_element_type=F32)
        acc = y if acc is None else acc + y
    return acc


def _sigmoid(x):
    return 1.0 / (1.0 + jnp.exp(-x))


def _norm_matmul_kernel(x_ref, g_ref, w_ref, o_ref, hn_ref):
    @pl.when(pl.program_id(1) == 0)
    def _():
        x = x_ref[...]
        ms = jnp.mean(x * x, axis=-1, keepdims=True)
        hn_ref[...] = ((x * lax.rsqrt(ms + EPS)) * g_ref[...]).astype(BF16)

    o_ref[...] = jnp.dot(hn_ref[...], w_ref[...], preferred_element_type=F32)


def _norm_matmul(x, g, w, *, tn, tc):
    n, d = x.shape
    c = w.shape[1]
    return pl.pallas_call(
        _norm_matmul_kernel,
        grid=(n // tn, c // tc),
        in_specs=[
            pl.BlockSpec((tn, d), lambda i, j: (i, 0)),
            pl.BlockSpec((1, d), lambda i, j: (0, 0)),
            pl.BlockSpec((d, tc), lambda i, j: (0, j)),
        ],
        out_specs=pl.BlockSpec((tn, tc), lambda i, j: (i, j)),
        out_shape=jax.ShapeDtypeStruct((n, c), F32),
        scratch_shapes=[pltpu.VMEM((tn, d), BF16)],
        compiler_params=_cparams(("parallel", "arbitrary"), vmem_mb=48),
        name="norm_matmul",
    )(x, g.reshape(1, d), w)


def _group_mean_matrix(width, group):
    r = np.arange(width)
    return jnp.asarray((r[:, None] // group == r[None, :] // group) / group, BF16)


def _prep_kernel(dq_ref, dk_ref, dv_ref, mq_ref, gq_ref, gk_ref, gm_ref, bd64_ref, bd128_ref,
                 qn_ref, kn_ref, knb_ref, vb_ref, qm_ref):
    def gnorm(x, bd, g):
        ms = _split_dot(x * x, bd, 3)
        return (x * lax.rsqrt(ms + EPS)) * g

    q = gnorm(dq_ref[...], bd64_ref[...], gq_ref[...])
    qn_ref[...] = (q * (DA_DK ** -0.5 * LOG2_E)).astype(BF16)
    k = gnorm(dk_ref[...], bd64_ref[...], gk_ref[...])
    kn_ref[...] = k
    knb_ref[...] = k.astype(BF16)
    vb_ref[...] = dv_ref[...].astype(BF16)
    qm_ref[...] = gnorm(mq_ref[...], bd128_ref[...], gm_ref[...]).astype(BF16)


def _prep(proj, gq, gk, gm, *, tn):
    n = proj.shape[0]
    w = BRANCH_W
    col = lambda c: pl.BlockSpec((tn, w), lambda i, c=c: (i, c))
    const = lambda shape: pl.BlockSpec(shape, lambda i: (0, 0))
    row_out = pl.BlockSpec((tn, w), lambda i: (i, 0))
    tile = lambda g: jnp.tile(g.astype(F32), w // g.shape[0]).reshape(1, w)
    return pl.pallas_call(
        _prep_kernel,
        grid=(n // tn,),
        in_specs=[col(4), col(5), col(6), col(7), const((1, w)), const((1, w)), const((1, w)),
                  const((w, w)), const((w, w))],
        out_specs=[row_out] * 5,
        out_shape=[jax.ShapeDtypeStruct((n, w), BF16), jax.ShapeDtypeStruct((n, w), F32),
                   jax.ShapeDtypeStruct((n, w), BF16), jax.ShapeDtypeStruct((n, w), BF16),
                   jax.ShapeDtypeStruct((n, w), BF16)],
        compiler_params=_cparams(("parallel",)),
        name="qk_norm",
    )(proj, proj, proj, proj, tile(gq), tile(gk), tile(gm),
      _group_mean_matrix(w, DA_DK), _group_mean_matrix(w, HEAD_W))


def _hgrn_kernel(*refs, layer, nb, tc, c, tvalid, has_s0):
    if has_s0:
        hq_ref, hf_ref, hi_ref, hg_ref, lbl_ref, og_ref, tri_ref, s0_ref = refs[:8]
        rest = refs[8:]
    else:
        hq_ref, hf_ref, hi_ref, hg_ref, lbl_ref, og_ref, tri_ref = refs[:7]
        s0_ref = None
        rest = refs[7:]
    o_ref, sout_ref, st_ref, q_s, k_s, b_s = rest
    t = pl.program_id(1)
    rows = nb * tc

    @pl.when(t == 0)
    def _():
        if has_s0:
            for ib in range(nb):
                for h in range(HEADS):
                    st_ref[ib, h] = s0_ref[ib, h].T
        else:
            st_ref[...] = jnp.zeros_like(st_ref)

    lg = lbl_ref[...]
    e = jnp.exp(lg - jnp.max(lg, axis=0, keepdims=True))
    pr = e / jnp.sum(e, axis=0, keepdims=True)
    lb = jnp.zeros((1, BRANCH_W), F32)
    for i in range(1, layer + 1):
        lb = lb + pr[i:i + 1]
    lb = jnp.clip(lb, 0.0, 1.0 - 1e-4)

    z = hf_ref[...]
    sg = _sigmoid(z)
    logf = jnp.log(lb + (1.0 - lb) * sg)
    kk = (1.0 - lb) * _sigmoid(-z)
    if tvalid < c:
        valid = (lax.broadcasted_iota(I32, (rows, BRANCH_W), 0) % c) < tvalid
        logf = jnp.where(valid, logf, 0.0)
        kk = jnp.where(valid, kk, 0.0)
    hq = hq_ref[...]
    q_s[...] = hq * _sigmoid(hq)
    k_s[...] = kk
    b_s[...] = _split_dot(logf, tri_ref[...], 3, left=True)

    og = og_ref[...]
    row_id = lax.broadcasted_iota(I32, (c, HEAD_W), 0)
    nsub = tc // c

    def sub(i, carry):
        r0 = pl.multiple_of(i * c, c)
        ib = i // nsub
        first = (i % nsub) == 0
        new_carry = []
        for h in range(HEADS):
            sl = slice(h * HEAD_W, (h + 1) * HEAD_W)
            bg = b_s[pl.ds(r0, c), sl]
            new_carry.append(bg[c - 1:c, :])
            b = bg - jnp.where(first, 0.0, carry[h])
            q = q_s[pl.ds(r0, c), sl]
            kx = k_s[pl.ds(r0, c), sl]
            iv = hi_ref[pl.ds(r0, c), sl]
            st = st_ref[ib, h]
            o_inter = lax.dot_general((q * jnp.exp(b)).astype(BF16), st.astype(BF16), NT_DIMS,
                                      preferred_element_type=F32)
            o_rows = []
            for tt in range(c):
                d = b[tt:tt + 1, :] - b
                w = jnp.exp(jnp.where(row_id <= tt, d, MASK_VALUE))
                p = (q[tt:tt + 1, :] * kx) * w
                a = jnp.sum(p, axis=-1, keepdims=True)
                o_rows.append(jnp.sum(a * iv, axis=0, keepdims=True))
            o = o_inter + jnp.concatenate(o_rows, axis=0)
            bl = b[c - 1:c, :]
            kd = kx * jnp.exp(bl - b)
            upd = lax.dot_general(iv.astype(BF16), kd.astype(BF16), TN_DIMS, preferred_element_type=F32)
            st_ref[ib, h] = jnp.exp(bl) * st + upd
            ms = jnp.mean(o * o, axis=-1, keepdims=True)
            hg = hg_ref[pl.ds(r0, c), sl]
            o_ref[pl.ds(r0, c), sl] = (((o * lax.rsqrt(ms + EPS)) * og) * (hg * _sigmoid(hg))).astype(o_ref.dtype)
        return tuple(new_carry)

    lax.fori_loop(0, nb * nsub, sub, tuple(jnp.zeros((1, HEAD_W), F32) for _ in range(HEADS)))

    @pl.when(t == pl.num_programs(1) - 1)
    def _():
        for ib in range(nb):
            for h in range(HEADS):
                sout_ref[ib, h] = st_ref[ib, h].T


def _hgrn(src, lb_logits, onorm_g, s0_all, *, layer, n_seq, seq_len, rows_out, row0, nb, tc, c, tvalid,
          out_dtype):
    rows = nb * tc
    nt = seq_len // tc
    rb0 = row0 // rows
    w = BRANCH_W
    col = lambda cc: pl.BlockSpec((rows, w), lambda b, t, cc=cc: (rb0 + b * nt + t, cc))
    r = np.arange(rows)
    tri = jnp.asarray((r[:, None] >= r[None, :]) & (r[:, None] // tc == r[None, :] // tc), BF16)
    in_specs = [col(0), col(1), col(2), col(3),
                pl.BlockSpec((DEPTH, w), lambda b, t: (0, 0)),
                pl.BlockSpec((1, HEAD_W), lambda b, t: (0, 0)),
                pl.BlockSpec((rows, rows), lambda b, t: (0, 0))]
    args = [src, src, src, src, lb_logits, onorm_g.reshape(1, HEAD_W), tri]
    has_s0 = s0_all is not None
    if has_s0:
        in_specs.append(pl.BlockSpec((None, nb, HEADS, HEAD_W, HEAD_W), lambda b, t: (layer, b, 0, 0, 0)))
        args.append(s0_all)
    kern = functools.partial(_hgrn_kernel, layer=layer, nb=nb, tc=tc, c=c, tvalid=tvalid, has_s0=has_s0)
    return pl.pallas_call(
        kern,
        grid=(n_seq // nb, nt),
        in_specs=in_specs,
        out_specs=[pl.BlockSpec((rows, w), lambda b, t: (b * nt + t, 0)),
                   pl.BlockSpec((nb, HEADS, HEAD_W, HEAD_W), lambda b, t: (b, 0, 0, 0))],
        out_shape=[jax.ShapeDtypeStruct((rows_out, w), out_dtype),
                   jax.ShapeDtypeStruct((n_seq, HEADS, HEAD_W, HEAD_W), F32)],
        scratch_shapes=[pltpu.VMEM((nb, HEADS, HEAD_W, HEAD_W), F32),
                        pltpu.VMEM((rows, w), F32), pltpu.VMEM((rows, w), F32), pltpu.VMEM((rows, w), F32)],
        compiler_params=_cparams(("parallel", "arbitrary")),
        name="hgrn2",
    )(*args)


def _lambda_value(lp, lam_init):
    a = jnp.sum(lp[0:1] * lp[1:2], axis=-1, keepdims=True)
    b = jnp.sum(lp[2:3] * lp[3:4], axis=-1, keepdims=True)
    return jnp.exp(a) - jnp.exp(b) + lam_init


def _dattn_prompt_kernel(iq_tab, ik_tab, lam_ref, q_ref, k_ref, v_ref, sg_ref, o_ref, q2_s, m_s, acc_s,
                         *, tq, rc, lam_init):
    step = pl.program_id(2)
    iq = iq_tab[step]
    ik = ik_tab[step]

    @pl.when(ik == 0)
    def _():
        q = q_ref[...].astype(F32)
        lane = lax.broadcasted_iota(I32, q.shape, 1)
        q2_s[0:tq, :] = jnp.where(lane < DA_DK, q, 0.0).astype(BF16)
        q2_s[tq:2 * tq, :] = jnp.where(lane >= DA_DK, q, 0.0).astype(BF16)
        m_s[...] = jnp.full_like(m_s, NEG_INF)
        acc_s[...] = jnp.zeros_like(acc_s)

    def update(masked):
        v1 = jnp.concatenate([v_ref[...], jnp.ones((tq, LANES), BF16)], axis=1)
        for c0 in range(0, 2 * tq, rc):
            rows = slice(c0, c0 + rc)
            nk = (c0 % tq) + rc if masked else tq
            s = lax.dot_general(q2_s[rows, :], k_ref[0:nk, :], NT_DIMS, preferred_element_type=F32)
            if masked:
                r = lax.broadcasted_iota(I32, s.shape, 0) + (c0 % tq)
                cidx = lax.broadcasted_iota(I32, s.shape, 1)
                s = jnp.where(cidx <= r, s, MASK_VALUE)
            m_prev = m_s[rows, :]
            m_new = jnp.maximum(m_prev, jnp.max(s, axis=-1, keepdims=True))
            alpha = jnp.exp2(m_prev - m_new)
            p = jnp.exp2(s - jnp.tile(m_new, (1, nk // LANES)))
            acc_s[rows, :] = (jnp.tile(alpha, (1, 2)) * acc_s[rows, :]
                              + jnp.dot(p.astype(BF16), v1[0:nk, :], preferred_element_type=F32))
            m_s[rows, :] = m_new

    @pl.when(ik < iq)
    def _():
        update(False)

    @pl.when(ik == iq)
    def _():
        update(True)
        lam = _lambda_value(lam_ref[...], lam_init)
        o0 = acc_s[0:tq, 0:HEAD_W] / acc_s[0:tq, HEAD_W:]
        o1 = acc_s[tq:2 * tq, 0:HEAD_W] / acc_s[tq:2 * tq, HEAD_W:]
        da = o0 - lam * o1
        ms = jnp.mean(da * da, axis=-1, keepdims=True)
        o_ref[...] = (((da * lax.rsqrt(ms + EPS)) * sg_ref[...]) * (1.0 - lam_init)).astype(o_ref.dtype)


def _dattn_prompt(qn, knb, vb, lam_p, subln_g, *, n_seq, seq_len, rows_out, tq, rc, lam_init):
    nq = seq_len // tq
    pairs = [(iq, ik) for iq in range(nq) for ik in range(iq + 1)]
    iq_tab = jnp.asarray([p[0] for p in pairs], I32)
    ik_tab = jnp.asarray([p[1] for p in pairs], I32)
    qspec = pl.BlockSpec((tq, HEAD_W), lambda b, h, s, iqt, ikt: (b * nq + iqt[s], h))
    kspec = pl.BlockSpec((tq, HEAD_W), lambda b, h, s, iqt, ikt: (b * nq + ikt[s], h))
    grid_spec = pltpu.PrefetchScalarGridSpec(
        num_scalar_prefetch=2,
        grid=(n_seq, HEADS, len(pairs)),
        in_specs=[pl.BlockSpec((4, DA_DK), lambda b, h, s, iqt, ikt: (0, 0)), qspec, kspec, kspec,
                  pl.BlockSpec((1, HEAD_W), lambda b, h, s, iqt, ikt: (0, 0))],
        out_specs=qspec,
        scratch_shapes=[pltpu.VMEM((2 * tq, HEAD_W), BF16), pltpu.VMEM((2 * tq, LANES), F32),
                        pltpu.VMEM((2 * tq, HEAD_W + LANES), F32)],
    )
    return pl.pallas_call(
        functools.partial(_dattn_prompt_kernel, tq=tq, rc=rc, lam_init=lam_init),
        grid_spec=grid_spec,
        out_shape=jax.ShapeDtypeStruct((rows_out, BRANCH_W), BF16),
        compiler_params=_cparams(("parallel", "parallel", "arbitrary")),
        name="diff_attn_prompt",
    )(iq_tab, ik_tab, lam_p, qn, knb, vb, subln_g.reshape(1, HEAD_W))


def _dattn_sample_kernel(pt_ref, lam_ref, qr_ref, kc_ref, vc_ref, sg_ref, *rest, n_pages, t_new, lam_init):
    k_refs = rest[:n_pages]
    v_refs = rest[n_pages:2 * n_pages]
    o_ref = rest[2 * n_pages]
    nrow = 2 * HEADS * t_new
    qr = qr_ref[...]
    s_past = jnp.concatenate(
        [jnp.dot(qr, k[...].astype(BF16), preferred_element_type=F32) for k in k_refs], axis=1)

    qf = qr.astype(F32)
    kc = kc_ref[...]
    vc = vc_ref[...]
    t_of_row = lax.broadcasted_iota(I32, (nrow, 1), 0) % t_new
    s_new = []
    for j in range(t_new):
        sj = jnp.sum(qf * kc[j:j + 1, :], axis=-1, keepdims=True)
        s_new.append(jnp.where(t_of_row >= j, sj, MASK_VALUE))
    m = jnp.max(s_past, axis=-1, keepdims=True)
    for sj in s_new:
        m = jnp.maximum(m, sj)
    p_past = jnp.exp2(s_past - m)
    p_new = [jnp.exp2(sj - m) for sj in s_new]
    l = jnp.sum(p_past, axis=-1, keepdims=True)
    for pj in p_new:
        l = l + pj

    lam = _lambda_value(lam_ref[...], lam_init)
    rows_h = 2 * t_new
    for h in range(HEADS):
        sl = slice(h * HEAD_W, (h + 1) * HEAD_W)
        r0 = h * rows_h
        v_h = jnp.concatenate([v[pl.ds(h, PAGE_SIZE, stride=HEADS), :] for v in v_refs], axis=0)
        acc = jnp.dot(p_past[r0:r0 + rows_h].astype(BF16), v_h.astype(BF16), preferred_element_type=F32)
        for j in range(t_new):
            acc = acc + p_new[j][r0:r0 + rows_h] * vc[j:j + 1, sl]
        on = acc / l[r0:r0 + rows_h]
        da = on[:t_new] - lam * on[t_new:]
        ms = jnp.mean(da * da, axis=-1, keepdims=True)
        o_ref[:, sl] = (((da * lax.rsqrt(ms + EPS)) * sg_ref[...]) * (1.0 - lam_init)).astype(o_ref.dtype)


def _dattn_sample(page_table, qrows, kcur, vcur, cache_kt, cache_v, lam_p, subln_g, *, layer, lam_init):
    n_seq, n_pages = page_table.shape
    t_new = kcur.shape[1]
    nrow = qrows.shape[1]
    per_seq = lambda shape: pl.BlockSpec((None,) + shape, lambda b, pt: (b, 0, 0))
    page = lambda p: pl.BlockSpec((None, None, PAGE_SIZE * HEADS, HEAD_W),
                                  lambda b, pt, p=p: (layer, pt[b * n_pages + p], 0, 0))
    grid_spec = pltpu.PrefetchScalarGridSpec(
        num_scalar_prefetch=1,
        grid=(n_seq,),
        in_specs=[pl.BlockSpec((4, DA_DK), lambda b, pt: (0, 0)),
                  per_seq((nrow, BRANCH_W)), per_seq((t_new, BRANCH_W)), per_seq((t_new, BRANCH_W)),
                  pl.BlockSpec((1, HEAD_W), lambda b, pt: (0, 0))]
                 + [page(p) for p in range(n_pages)] * 2,
        out_specs=per_seq((t_new, BRANCH_W)),
    )
    return pl.pallas_call(
        functools.partial(_dattn_sample_kernel, n_pages=n_pages, t_new=t_new, lam_init=lam_init),
        grid_spec=grid_spec,
        out_shape=jax.ShapeDtypeStruct((n_seq, t_new, BRANCH_W), F32),
        compiler_params=_cparams(("parallel",), vmem_mb=40),
        name="diff_attn_sample",
    )(page_table.reshape(-1), lam_p, qrows, kcur, vcur, subln_g.reshape(1, HEAD_W),
      *([cache_kt] * n_pages), *([cache_v] * n_pages))


def _mem_kv_kernel(m_ref, g_ref, w_ref, kg_ref, bd_ref, mk_ref, mv_ref):
    x = m_ref[...]
    ms = jnp.mean(x * x, axis=-1, keepdims=True)
    hn = ((x * lax.rsqrt(ms + EPS)) * g_ref[...]).astype(BF16)
    kv = jnp.dot(hn, w_ref[...], preferred_element_type=F32)
    k = kv[:, :BRANCH_W]
    ms = _split_dot(k * k, bd_ref[...], 3)
    mk_ref[...] = (k * lax.rsqrt(ms + EPS)) * kg_ref[...]
    mv_ref[...] = kv[:, BRANCH_W:]


def _mem_kv(mem, norm_g, w_kv, knorm_g):
    n_seq = mem.shape[0]
    w = BRANCH_W
    out = pl.BlockSpec((None, None, MEM_LEN, w), lambda l, b: (l, b, 0, 0))
    return pl.pallas_call(
        _mem_kv_kernel,
        grid=(DEPTH, n_seq),
        in_specs=[pl.BlockSpec((None, MEM_LEN, D_MODEL), lambda l, b: (b, 0, 0)),
                  pl.BlockSpec((None, 1, D_MODEL), lambda l, b: (l, 0, 0)),
                  pl.BlockSpec((None, D_MODEL, 2 * w), lambda l, b: (l, 0, 0)),
                  pl.BlockSpec((None, 1, w), lambda l, b: (l, 0, 0)),
                  pl.BlockSpec((w, w), lambda l, b: (0, 0))],
        out_specs=[out, out],
        out_shape=[jax.ShapeDtypeStruct((DEPTH, n_seq, MEM_LEN, w), F32)] * 2,
        compiler_params=_cparams(("parallel", "parallel")),
        name="mem_kv",
    )(mem, norm_g.reshape(DEPTH, 1, D_MODEL), w_kv,
      jnp.tile(knorm_g.astype(F32), (1, HEADS)).reshape(DEPTH, 1, w), _group_mean_matrix(w, HEAD_W))


def _cross_kernel(q_ref, mk_ref, mv_ref, o_ref, *, head_rows, n_seq_blk):
    tq = q_ref.shape[0] // n_seq_blk
    for r in range(n_seq_blk):
        q = q_ref[r * tq:(r + 1) * tq, :]
        for h in range(HEADS):
            sl = slice(h * HEAD_W, (h + 1) * HEAD_W)
            if head_rows:
                m0 = r * MEM_LEN * HEADS + h
                mk = mk_ref[pl.ds(m0, MEM_LEN, stride=HEADS), :].astype(BF16)
                mv = mv_ref[pl.ds(m0, MEM_LEN, stride=HEADS), :].astype(BF16)
            else:
                mk = mk_ref[r * MEM_LEN:(r + 1) * MEM_LEN, sl].astype(BF16)
                mv = mv_ref[r * MEM_LEN:(r + 1) * MEM_LEN, sl].astype(BF16)
            s = lax.dot_general(q[:, sl], mk, NT_DIMS, preferred_element_type=F32) * (HEAD_W ** -0.5)
            p = jnp.exp(s - jnp.max(s, axis=-1, keepdims=True))
            l = jnp.sum(p, axis=-1, keepdims=True)
            o = jnp.dot(p.astype(BF16), mv, preferred_element_type=F32) / l
            o_ref[r * tq:(r + 1) * tq, sl] = o.astype(o_ref.dtype)


def _cross_prompt(qm, mk_all, mv_all, *, layer, n_seq, seq_len, rows_out, tq):
    nq = seq_len // tq
    mem = pl.BlockSpec((None, None, MEM_LEN, BRANCH_W), lambda b, i: (layer, b, 0, 0))
    row = pl.BlockSpec((tq, BRANCH_W), lambda b, i: (b * nq + i, 0))
    return pl.pallas_call(
        functools.partial(_cross_kernel, head_rows=False, n_seq_blk=1),
        grid=(n_seq, nq),
        in_specs=[row, mem, mem],
        out_specs=row,
        out_shape=jax.ShapeDtypeStruct((rows_out, BRANCH_W), BF16),
        compiler_params=_cparams(("parallel", "parallel")),
        name="cross_attn_prompt",
    )(qm, mk_all, mv_all)


def _cross_sample(qm_pad, mk_all, mv_all, *, layer):
    n_seq, tpad, w = qm_pad.shape
    nb = CROSS_SAMPLE_SEQS
    depth = mk_all.shape[0]
    mem_rows = nb * MEM_LEN * HEADS
    mem = pl.BlockSpec((None, None, mem_rows, HEAD_W), lambda b: (layer, b, 0, 0))
    row = pl.BlockSpec((None, nb * tpad, w), lambda b: (b, 0, 0))
    grouped = lambda m: m.reshape(depth, n_seq // nb, mem_rows, HEAD_W)
    return pl.pallas_call(
        functools.partial(_cross_kernel, head_rows=True, n_seq_blk=nb),
        grid=(n_seq // nb,),
        in_specs=[row, mem, mem],
        out_specs=row,
        out_shape=jax.ShapeDtypeStruct((n_seq // nb, nb * tpad, w), F32),
        compiler_params=_cparams(("parallel",)),
        name="cross_attn_sample",
    )(qm_pad.reshape(n_seq // nb, nb * tpad, w), grouped(mk_all), grouped(mv_all)).reshape(n_seq, tpad, w)


def _merge_kernel(x_ref, g0_ref, g1_ref, g2_ref, a0_ref, a1_ref, a2_ref, w0_ref, w1_ref, w2_ref, wo_ref, o_ref):
    merged = None
    for g_ref, a_ref, w_ref in ((g0_ref, a0_ref, w0_ref), (g1_ref, a1_ref, w1_ref), (g2_ref, a2_ref, w2_ref)):
        y = _sigmoid(g_ref[...]) * jnp.dot(a_ref[...], w_ref[...], preferred_element_type=F32)
        merged = y if merged is None else merged + y
    o_ref[...] = x_ref[...] + jnp.dot(merged.astype(BF16), wo_ref[...], preferred_element_type=F32)


def _merge(x, proj, hg, da, cm, w_hg, w_da, w_mem, w_out, *, tn):
    n, d = x.shape
    w = BRANCH_W
    gate_col0 = (8 * w) // d
    row = lambda width: pl.BlockSpec((tn, width), lambda i: (i, 0))
    gate = lambda c: pl.BlockSpec((tn, d), lambda i, c=c: (i, gate_col0 + c))
    const = lambda shape: pl.BlockSpec(shape, lambda i: (0, 0))
    return pl.pallas_call(
        _merge_kernel,
        grid=(n // tn,),
        in_specs=[row(d), gate(0), gate(1), gate(2), row(w), row(w), row(w),
                  const((w, d)), const((w, d)), const((w, d)), const((d, d))],
        out_specs=row(d),
        out_shape=jax.ShapeDtypeStruct((n, d), F32),
        compiler_params=_cparams(("parallel",), vmem_mb=48),
        name="branch_merge",
    )(x, proj, proj, proj, hg, da, cm, w_hg, w_da, w_mem, w_out)


def _topk_rows(scores, k, key_id):
    scores = list(scores)
    vals = [[] for _ in scores]
    ids = [[] for _ in scores]
    for _ in range(k):
        for n, s in enumerate(scores):
            m = jnp.max(s, axis=0, keepdims=True)
            sel = jnp.min(jnp.where(s == m, key_id, _ID_SENTINEL), axis=0, keepdims=True)
            scores[n] = jnp.where(key_id == sel, NEG_INF, s)
            vals[n].append(m)
            ids[n].append(sel)
        yield None
    return vals, ids


def _candidate_tables(tt):
    flat = np.full((_CAND_ROWS,), _ID_SENTINEL, np.float64)
    flat[0:16] = np.arange(16)
    flat[16:24] = 16 + np.arange(8)
    for j1 in range(2, 8):
        lim = PK_TOPK // (j1 + 1)
        flat[24 + (j1 - 2) * 8:24 + (j1 - 2) * 8 + lim] = j1 * 16 + np.arange(lim)
    flat[72:80] = (8 + np.arange(8)) * 16
    return jnp.asarray(np.broadcast_to(flat[:, None], (_CAND_ROWS, tt)), F32)


def _norm_bf16(x, g):
    ms = jnp.mean(x * x, axis=-1, keepdims=True)
    return ((x * lax.rsqrt(ms + EPS)) * g).astype(BF16)


def _select_head_steps(h, hn, wq_ref, sk_ref, cflat, eid_t, gate_t, lane0=0):
    tt = hn.shape[0]
    cflat = cflat[:, :tt]
    key_id = lax.broadcasted_iota(I32, (N_KEYS, tt), 0).astype(F32)
    cvalid = cflat < float(PK_TOPK * PK_TOPK)
    qh = jnp.dot(hn, wq_ref[h], preferred_element_type=F32).astype(BF16)
    sc = lax.dot_general(sk_ref[h], qh, NT_DIMS, preferred_element_type=F32)
    yield None
    (v1, v2), (i1, i2) = yield from _topk_rows([sc[:N_KEYS], sc[N_KEYS:]], PK_TOPK, key_id)
    v2a = jnp.concatenate(v2, axis=0)
    i2a = jnp.concatenate(i2, axis=0)
    pieces_v = [v1[0] + v2a, v1[1] + v2a[:8]]
    pieces_e = [i1[0] * N_KEYS + i2a, i1[1] * N_KEYS + i2a[:8]]
    for j1 in range(2, 8):
        pieces_v.append(v1[j1] + v2a[:8])
        pieces_e.append(i1[j1] * N_KEYS + i2a[:8])
    pieces_v.append(jnp.concatenate(v1[8:], axis=0) + v2a[:1])
    pieces_e.append(jnp.concatenate(i1[8:], axis=0) * N_KEYS + i2a[:1])
    cand = jnp.where(cvalid, jnp.concatenate(pieces_v, axis=0), NEG_INF)
    ce = jnp.concatenate(pieces_e, axis=0)
    fv, fe = [], []
    for _ in range(PK_TOPK):
        m = jnp.max(cand, axis=0, keepdims=True)
        sel = jnp.min(jnp.where(cand == m, cflat, _ID_SENTINEL), axis=0, keepdims=True)
        hit = cflat == sel
        fe.append(jnp.max(jnp.where(hit, ce, -1.0), axis=0, keepdims=True))
        cand = jnp.where(hit, NEG_INF, cand)
        fv.append(m)
        yield None
    fva = jnp.concatenate(fv, axis=0)
    ex = jnp.exp(fva - fva[0:1])
    r0 = pl.multiple_of(h * PK_TOPK, PK_TOPK)
    gate_t[pl.ds(r0, PK_TOPK), lane0:lane0 + tt] = ex / jnp.sum(ex, axis=0, keepdims=True)
    eid_t[pl.ds(r0, PK_TOPK), lane0:lane0 + tt] = jnp.concatenate(fe, axis=0)


def _select_head(*args):
    for _ in _select_head_steps(*args):
        pass


def _build_routing(eid_s, gate_s, w_s, *, tn, pitch):
    n_sel = PK_HEADS * PK_TOPK
    row_id = lax.broadcasted_iota(I32, (N_KEYS, n_sel), 0).astype(F32).astype(BF16)
    one = jnp.ones((N_KEYS, n_sel), BF16)
    zero = jnp.zeros((N_KEYS, n_sel), BF16)
    bf16_rows = 16

    def all_rows(v):
        return jnp.tile(jnp.broadcast_to(v, (bf16_rows, n_sel)).astype(BF16), (N_KEYS // bf16_rows, 1))

    group = 64

    def build(n8, carry):
        n0 = pl.multiple_of(n8 * group, group)
        e8 = eid_s[pl.ds(n0, group), :]
        g8 = gate_s[pl.ds(n0, group), :]
        for r in range(group):
            e = e8[r:r + 1, :]
            g = 0.5 * g8[r:r + 1, :]
            a_id = jnp.floor(e * (1.0 / N_KEYS))
            sel_a = jnp.where(row_id == all_rows(a_id), one, zero)
            hit_b = row_id == all_rows(e - a_id * N_KEYS)
            g_hi = g.astype(BF16).astype(F32)
            g_lo = g - g_hi
            lhs = jnp.concatenate([sel_a, sel_a], axis=1)
            rhs = jnp.concatenate([jnp.where(hit_b, all_rows(g_hi), zero),
                                   jnp.where(hit_b, all_rows(g_lo), zero)], axis=1)
            w = lax.dot_general(lhs, rhs, NT_DIMS, preferred_element_type=F32)
            w_s[pl.ds(n0 + r, N_KEYS, stride=pitch), :] = w
        return carry

    lax.fori_loop(0, tn // group, build, 0)


def _peer_kernel(xc_ref, xn_ref, g_ref, wq_ref, sk_ref, cflat_ref, u_ref, v_ref, o_ref,
                 w_s, acc_s, hn_s, eid_t, gate_t, eid_s, gate_s, *, tn, te, pitch):
    i = pl.program_id(0)
    j = pl.program_id(1)
    cur = i % 2
    nxt = 1 - cur
    cflat = cflat_ref[...]

    @pl.when(j == 0)
    def _():
        @pl.when(i == 0)
        def _():
            hn0 = _norm_bf16(xc_ref[...], g_ref[...])
            hn_s[0] = hn0

            def head(h, carry):
                _select_head(h, hn0, wq_ref, sk_ref, cflat, eid_t, gate_t)
                return carry

            lax.fori_loop(0, PK_HEADS, head, 0)
            eid_s[...] = eid_t[...].T
            gate_s[...] = gate_t[...].T

        _build_routing(eid_s, gate_s, w_s, tn=tn, pitch=pitch)
        acc_s[...] = jnp.zeros_like(acc_s)
        hn_s[nxt] = _norm_bf16(xn_ref[...], g_ref[...])

    sub = LANES
    hn_next = hn_s[nxt]
    selects = [_select_head_steps(j, hn_next[t0:t0 + sub], wq_ref, sk_ref, cflat, eid_t, gate_t, lane0=t0)
               for t0 in range(0, tn, sub)]

    def advance(units):
        for _ in range(units):
            while selects and next(selects[0], "done") == "done":
                selects.pop(0)
                if not selects:
                    return

    advance(1)
    units_per_piece = 8
    hn = hn_s[cur]
    acc = acc_s[...]
    e_chunk = 2 * LANES
    v_chunk = 8 * e_chunk
    for c0 in range(0, te, v_chunk):
        pieces = []
        for c1 in range(c0, c0 + v_chunk, e_chunk):
            a = lax.dot_general(hn, u_ref[c1:c1 + e_chunk, :], NT_DIMS, preferred_element_type=F32)
            advance(units_per_piece // 2)
            for r in range(e_chunk // LANES):
                ar = a[:, r * LANES:(r + 1) * LANES]
                a_id = j * (te // LANES) + c1 // LANES + r
                wr = w_s[pl.ds(pl.multiple_of(a_id * pitch, 8), tn), :]
                act = ar * (1.0 + lax.erf(ar * (0.5 ** 0.5)))
                pieces.append((wr * act).astype(BF16))
            advance(units_per_piece - units_per_piece // 2)
        acc = acc + jnp.dot(jnp.concatenate(pieces, axis=1), v_ref[c0:c0 + v_chunk, :],
                            preferred_element_type=F32)
        advance(units_per_piece)
    acc_s[...] = acc
    advance(len(selects) * 64)

    @pl.when(j == pl.num_programs(1) - 1)
    def _():
        o_ref[...] = xc_ref[...] + acc_s[...]
        eid_s[...] = eid_t[...].T
        gate_s[...] = gate_t[...].T


def _peer(x, g, wq_heads, sk_pad, u_all, v_all, *, layer, tn, te):
    n, d = x.shape
    nt = n // tn
    assert N_EXPERTS // te == PK_HEADS
    sel = PK_HEADS * PK_TOPK
    pitch = tn + 8
    table = pl.BlockSpec((None, te, d), lambda i, j: (layer, j, 0))
    const = lambda shape: pl.BlockSpec(shape, lambda i, j: (0,) * len(shape))
    return pl.pallas_call(
        functools.partial(_peer_kernel, tn=tn, te=te, pitch=pitch),
        grid=(nt, N_EXPERTS // te),
        in_specs=[pl.BlockSpec((tn, d), lambda i, j: (i, 0)),
                  pl.BlockSpec((tn, d), lambda i, j: (jnp.minimum(i + 1, nt - 1), 0)),
                  const((1, d)), const((PK_HEADS, d, LANES)), const((PK_HEADS, 2 * N_KEYS, LANES)),
                  const((_CAND_ROWS, tn)), table, table],
        out_specs=pl.BlockSpec((tn, d), lambda i, j: (i, 0)),
        out_shape=jax.ShapeDtypeStruct((n, d), F32),
        scratch_shapes=[pltpu.VMEM((N_KEYS * pitch, LANES), F32), pltpu.VMEM((tn, d), F32),
                        pltpu.VMEM((2, tn, d), BF16),
                        pltpu.VMEM((sel, tn), F32), pltpu.VMEM((sel, tn), F32),
                        pltpu.VMEM((tn, sel), F32), pltpu.VMEM((tn, sel), F32)],
        compiler_params=_cparams(("arbitrary", "arbitrary"), vmem_mb=60),
        name="peer",
    )(x, x, g.reshape(1, d), wq_heads, sk_pad, _candidate_tables(tn), u_all, v_all)


def _pad_tokens(a, tpad):
    return jnp.pad(a, ((0, 0), (0, tpad - a.shape[1]), (0, 0)))


def kernel(x_prompt, x_sample, cache_diff_k, cache_diff_v, cache_mem_k, cache_mem_v, state_hgrn, page_table, mem_prompt, norm_mix_g, w_in, hg_lb_logits, hg_onorm_g, da_qnorm_g, da_knorm_g, da_lambda, da_subln_g, mem_norm_g, w_mem_kv, mem_qnorm_g, mem_knorm_g, w_branch_hg, w_branch_da, w_branch_mem, w_out, norm_ffn_g, peer_wq, peer_subkeys, peer_u, peer_v):
    n_p, seq, d = x_prompt.shape
    n_s, t_new, _ = x_sample.shape
    rows_p = n_p * seq
    rows_s = n_s * t_new
    w = BRANCH_W
    depth = w_in.shape[0]

    w_in_b = w_in.astype(BF16)
    w_kv_b = w_mem_kv.astype(BF16)
    w_hg_b, w_da_b, w_cm_b, w_out_b = (a.astype(BF16) for a in (w_branch_hg, w_branch_da, w_branch_mem, w_out))
    wq_heads = peer_wq.astype(BF16).reshape(depth, d, PK_HEADS, LANES).transpose(0, 2, 1, 3)
    sk = peer_subkeys.astype(BF16)
    zk = jnp.zeros_like(sk[:, :, 0])
    sk_pad = jnp.concatenate([jnp.concatenate([sk[:, :, 0], zk], axis=-1),
                              jnp.concatenate([zk, sk[:, :, 1]], axis=-1)], axis=2)
    u_b = peer_u.astype(BF16)
    v_b = peer_v.astype(BF16)
    n_pool = cache_diff_k.shape[1]
    cache_kt = jnp.transpose(cache_diff_k, (0, 1, 3, 4, 5, 2)).reshape(depth, n_pool, w, PAGE_SIZE)
    cache_v = cache_diff_v.reshape(depth, n_pool, PAGE_SIZE * HEADS, HEAD_W)
    cmem_k = cache_mem_k.reshape(depth, n_s, MEM_LEN * HEADS, HEAD_W)
    cmem_v = cache_mem_v.reshape(depth, n_s, MEM_LEN * HEADS, HEAD_W)
    lb_logits = hg_lb_logits.astype(F32)
    rows_all = rows_p + rows_s
    with_sample = lambda full, part: lax.dynamic_update_slice(full, part, (rows_p, 0))

    mk_p, mv_p = _mem_kv(mem_prompt, mem_norm_g, w_kv_b, mem_knorm_g)

    x = jnp.concatenate([x_prompt.reshape(rows_p, d), x_sample.reshape(rows_s, d)], axis=0)
    row_mask = (np.arange(2 * HEADS * t_new)[:, None] // t_new) == (np.arange(w)[None, :] // DA_DK)
    row_mask = jnp.asarray(row_mask)
    tpad_h = 8
    tpad_c = 16

    hg_p, hg_s, dk_p, dv_p, dk_s, dv_s = [], [], [], [], [], []
    for l in range(depth):
        lam_init = 0.8 - 0.6 * math.exp(-0.3 * l)
        lam_p = da_lambda[l].astype(F32)
        proj = _norm_matmul(x, norm_mix_g[l], w_in_b[l], tn=PROJ_ROW_TILE, tc=PROJ_COL_TILE)
        qn, kn, knb, vb, qm = _prep(proj, da_qnorm_g[l], da_knorm_g[l], mem_qnorm_g[l], tn=ROW_TILE)

        hgo_p, st_p = _hgrn(proj, lb_logits, hg_onorm_g[l], None, layer=l, n_seq=n_p, seq_len=seq,
                            rows_out=rows_all, row0=0, nb=1, tc=HGRN_BLOCK, c=HGRN_CHUNK, tvalid=HGRN_CHUNK,
                            out_dtype=BF16)
        src_s = _pad_tokens(proj[rows_p:, :4 * w].reshape(n_s, t_new, 4 * w), tpad_h).reshape(n_s * tpad_h, 4 * w)
        hgo_s, st_s = _hgrn(src_s, lb_logits, hg_onorm_g[l], state_hgrn, layer=l, n_seq=n_s, seq_len=tpad_h,
                            rows_out=n_s * tpad_h, row0=0, nb=8, tc=tpad_h, c=tpad_h, tvalid=t_new, out_dtype=F32)
        hgo_s = hgo_s.reshape(n_s, tpad_h, w)[:, :t_new].reshape(rows_s, w).astype(BF16)

        da_p = _dattn_prompt(qn, knb, vb, lam_p, da_subln_g[l], n_seq=n_p, seq_len=seq, rows_out=rows_all,
                             tq=ATTN_TILE, rc=ATTN_ROW_CHUNK, lam_init=lam_init)
        qs = qn[rows_p:].reshape(n_s, t_new, w)
        qrows = jnp.where(row_mask, jnp.tile(qs, (1, 2 * HEADS, 1)), jnp.zeros((), BF16))
        da_s = _dattn_sample(page_table, qrows, knb[rows_p:].reshape(n_s, t_new, w).astype(F32),
                             vb[rows_p:].reshape(n_s, t_new, w).astype(F32), cache_kt, cache_v, lam_p,
                             da_subln_g[l], layer=l, lam_init=lam_init)
        da_s = da_s.reshape(rows_s, w).astype(BF16)

        cm_p = _cross_prompt(qm, mk_p, mv_p, layer=l, n_seq=n_p, seq_len=seq, rows_out=rows_all, tq=ATTN_TILE)
        qm_s = _pad_tokens(qm[rows_p:].reshape(n_s, t_new, w), tpad_c)
        cm_s = _cross_sample(qm_s, cmem_k, cmem_v, layer=l)[:, :t_new].reshape(rows_s, w).astype(BF16)

        x = _merge(x, proj, with_sample(hgo_p, hgo_s), with_sample(da_p, da_s), with_sample(cm_p, cm_s),
                   w_hg_b[l], w_da_b[l], w_cm_b[l], w_out_b[l], tn=ROW_TILE)

        x = _peer(x, norm_ffn_g[l], wq_heads[l], sk_pad[l], u_b, v_b, layer=l,
                  tn=PEER_TOKEN_TILE, te=PEER_EXPERT_TILE)

        hg_p.append(st_p)
        hg_s.append(st_s)
        dk_p.append(kn[:rows_p].reshape(n_p, seq, HEADS, 2, DA_DK))
        dv_p.append(proj[:rows_p, 6 * w:7 * w].reshape(n_p, seq, HEADS, HEAD_W))
        dk_s.append(kn[rows_p:].reshape(n_s, t_new, HEADS, 2, DA_DK))
        dv_s.append(proj[rows_p:, 6 * w:7 * w].reshape(n_s, t_new, HEADS, HEAD_W))

    mem_shape = (depth, n_p, MEM_LEN, HEADS, HEAD_W)
    return (x[:rows_p].reshape(n_p, seq, d), x[rows_p:].reshape(n_s, t_new, d),
            jnp.stack(hg_p), jnp.stack(hg_s), jnp.stack(dk_p), jnp.stack(dv_p),
            jnp.stack(dk_s), jnp.stack(dv_s), mk_p.reshape(mem_shape), mv_p.reshape(mem_shape))
```
